```python
import math
import jax, jax.numpy as jnp
from jax import lax
import numpy as np

D_MODEL = 1024
BATCH = 8
SEQ = 2048
DEPTH = 1
DEC_BATCH = 16
DEC_SEQ = 64
PAST_LEN = 4096

CHUNK = 64
D_MIX = D_MODEL
D_SSM = D_MIX // 2
SSM_GROUP = 16
N_SSM_GROUPS = D_SSM // SSM_GROUP
SSM_STATE = 64
D_ATT = D_MIX - D_SSM
N_HEADS = 4
HEAD_DIM = D_ATT // (2 * N_HEADS)
V_DIM = 2 * HEAD_DIM
D_IN = D_SSM + 3 * D_ATT
N_MEM = 256
N_MEM_HEADS = 4
MEM_HEAD_DIM = D_MODEL // N_MEM_HEADS
D_FF = 2816
Q_BLOCK = 128
EPS = 1e-6

kernel_name = "hymba_s5_diffattn_streaming_step"

F32 = jnp.float32


def rmsnorm(x, g):
    xf = x.astype(F32)
    y = xf * lax.rsqrt(jnp.mean(xf * xf, axis=-1, keepdims=True) + EPS)
    return (y * g.astype(F32)).astype(x.dtype)


def swiglu(x, w_gu, w_d):
    gate, up = jnp.split(x @ w_gu, 2, axis=-1)
    return (jax.nn.silu(gate) * up) @ w_d


def ffn_half(x, g, w_gu, w_d):
    return x + 0.5 * swiglu(rmsnorm(x, g), w_gu, w_d)


def mix_inputs(h, g_mix, w_in):
    z = rmsnorm(h, g_mix) @ w_in
    bsz, L, _ = z.shape
    u = z[..., :D_SSM]
    q = z[..., D_SSM:D_SSM + D_ATT].reshape(bsz, L, N_HEADS, 2, HEAD_DIM)
    k = z[..., D_SSM + D_ATT:D_SSM + 2 * D_ATT].reshape(bsz, L, N_HEADS, 2 * HEAD_DIM)
    v = z[..., D_SSM + 2 * D_ATT:].reshape(bsz, L, N_HEADS, V_DIM)
    return u, q, k, v


def s5_mixer(u, h0_re, h0_im, a_re, a_im, log_dt, b_re, b_im, c_re, c_im, d_skip, w_glu, b_glu):
    bsz, L, _ = u.shape
    lam = lax.complex(a_re.astype(F32), a_im.astype(F32))
    dt = jnp.exp(log_dt.astype(F32))[:, None]
    a_bar = jnp.exp(lam * dt)
    b = lax.complex(b_re.astype(F32), b_im.astype(F32))
    b_bar = ((a_bar - 1.0) / lam)[..., None] * b
    ug = u.astype(F32).reshape(bsz, L, N_SSM_GROUPS, SSM_GROUP)
    bu = jnp.einsum('gph,blgh->blgp', b_bar, ug)
    h0 = lax.complex(h0_re.astype(F32), h0_im.astype(F32))
    bu = bu.at[:, 0].add(a_bar * h0)
    a_seq = jnp.broadcast_to(a_bar, bu.shape)

    def combine(left, right):
        a_l, b_l = left
        a_r, b_r = right
        return a_r * a_l, a_r * b_l + b_r

    _, hs = lax.associative_scan(combine, (a_seq, bu), axis=1)
    c = lax.complex(c_re.astype(F32), c_im.astype(F32))
    y = jnp.einsum('ghp,blgp->blgh', c, hs).real + d_skip.astype(F32).reshape(N_SSM_GROUPS, SSM_GROUP) * ug
    y = jax.nn.gelu(y.reshape(bsz, L, D_SSM))
    y = y * jax.nn.sigmoid(y @ w_glu.astype(F32) + b_glu.astype(F32))
    h_last = hs[:, -1]
    return y.astype(u.dtype), h_last.real, h_last.imag


def alibi_slopes():
    return 2.0 ** (-8.0 * jnp.arange(1, N_HEADS + 1, dtype=F32) / N_HEADS)


def diff_lambda(lq, lk, lam_init):
    lq = lq.astype(F32)
    lk = lk.astype(F32)
    return jnp.exp(jnp.sum(lq[0] * lk[0])) - jnp.exp(jnp.sum(lq[1] * lk[1])) + lam_init


def diff_attend(q, k, v, qpos, kpos, lam, mask):
    kk = k.reshape(k.shape[:3] + (2, HEAD_DIM))
    s = jnp.einsum('bqhme,bkhme->bhmqk', q.astype(F32), kk.astype(F32)) * (HEAD_DIM ** -0.5)
    dist = jnp.abs(qpos[:, None] - kpos[None, :]).astype(F32)
    s = s - alibi_slopes()[None, :, None, None, None] * dist
    if mask is not None:
        s = jnp.where(mask, s, -jnp.inf)
    p = jax.nn.softmax(s, axis=-1)
    w = p[:, :, 0] - lam * p[:, :, 1]
    return jnp.einsum('bhqk,bkhv->bqhv', w, v.astype(F32))


def subln(o, g_subln, lam_init):
    return rmsnorm(o, g_subln) * (1.0 - lam_init)


def diff_attn_prompt(q, k, v, lam, lam_init, g_subln):
    bsz, L = q.shape[:2]
    nblk = L // Q_BLOCK
    kpos = jnp.arange(L)
    qb = q.reshape(bsz, nblk, Q_BLOCK, N_HEADS, 2, HEAD_DIM).swapaxes(0, 1)

    def block(args):
        qi, i = args
        qpos = i * Q_BLOCK + jnp.arange(Q_BLOCK)
        mask = (kpos[None, :] // CHUNK) <= (qpos[:, None] // CHUNK)
        return diff_attend(qi, k, v, qpos, kpos, lam, mask)

    o = lax.map(block, (qb, jnp.arange(nblk)))
    o = o.swapaxes(0, 1).reshape(bsz, L, N_HEADS, V_DIM)
    return subln(o, g_subln, lam_init)


def diff_attn_sample(q, k_new, v_new, k_cache, v_cache, lam, lam_init, g_subln):
    past = k_cache.shape[1]
    n_new = q.shape[1]
    k_all = jnp.concatenate([k_cache, k_new.astype(k_cache.dtype)], axis=1)
    v_all = jnp.concatenate([v_cache, v_new.astype(v_cache.dtype)], axis=1)
    qpos = past + jnp.arange(n_new)
    kpos = jnp.arange(past + n_new)
    o = diff_attend(q, k_all, v_all, qpos, kpos, lam, None)
    return subln(o, g_subln, lam_init)


def merge_groups(y_ssm, y_att, w_out):
    bsz, L = y_ssm.shape[:2]
    cat = jnp.concatenate([y_ssm, y_att.reshape(bsz, L, D_ATT).astype(y_ssm.dtype)], axis=-1)
    return cat @ w_out


def mem_kv(mem, g_mem, w_ck, w_cv):
    bsz = mem.shape[0]
    m = rmsnorm(mem, g_mem)
    mk = (m @ w_ck).reshape(bsz, N_MEM, N_MEM_HEADS, MEM_HEAD_DIM)
    mv = (m @ w_cv).reshape(bsz, N_MEM, N_MEM_HEADS, MEM_HEAD_DIM)
    return mk, mv


def cross_attend(h, g_cross, w_cq, w_co, mk, mv):
    bsz, L, _ = h.shape
    q = (rmsnorm(h, g_cross) @ w_cq).reshape(bsz, L, N_MEM_HEADS, MEM_HEAD_DIM)
    s = jnp.einsum('bqhd,bkhd->bhqk', q.astype(F32), mk.astype(F32)) * (MEM_HEAD_DIM ** -0.5)
    p = jax.nn.softmax(s, axis=-1)
    o = jnp.einsum('bhqk,bkhd->bqhd', p, mv.astype(F32)).reshape(bsz, L, D_MODEL)
    return o.astype(h.dtype) @ w_co


def setup_inputs(seed: int = 0) -> dict:
    key = jax.random.key(seed)
    keys = iter(jax.random.split(key, 48))

    def nrm(shape, scale=1.0):
        return scale * jax.random.normal(next(keys), shape, F32)

    def gain(shape):
        return 1.0 + nrm(shape, 0.02)

    G, P, H = N_SSM_GROUPS, SSM_STATE, SSM_GROUP
    n_idx = jnp.arange(P, dtype=F32)
    inp = {}
    inp["x_prompt"] = nrm((BATCH, SEQ, D_MODEL))
    inp["x_sample"] = nrm((DEC_BATCH, DEC_SEQ, D_MODEL))
    inp["cache_attn_k"] = nrm((DEPTH, DEC_BATCH, PAST_LEN, N_HEADS, 2 * HEAD_DIM))
    inp["cache_attn_v"] = nrm((DEPTH, DEC_BATCH, PAST_LEN, N_HEADS, V_DIM))
    inp["state_s5_re"] = nrm((DEPTH, DEC_BATCH, G, P), 0.5)
    inp["state_s5_im"] = nrm((DEPTH, DEC_BATCH, G, P), 0.5)
    inp["cache_mem_k"] = nrm((DEPTH, DEC_BATCH, N_MEM, N_MEM_HEADS, MEM_HEAD_DIM))
    inp["cache_mem_v"] = nrm((DEPTH, DEC_BATCH, N_MEM, N_MEM_HEADS, MEM_HEAD_DIM))
    inp["mem_prompt"] = nrm((BATCH, N_MEM, D_MODEL))
    inp["g_ffn1"] = gain((DEPTH, D_MODEL))
    inp["w_ffn1_gu"] = nrm((DEPTH, D_MODEL, 2 * D_FF), D_MODEL ** -0.5)
    inp["w_ffn1_d"] = nrm((DEPTH, D_FF, D_MODEL), D_FF ** -0.5)
    inp["g_mix"] = gain((DEPTH, D_MODEL))
    inp["w_in"] = nrm((DEPTH, D_MODEL, D_IN), D_MODEL ** -0.5)
    inp["ssm_a_re"] = -0.5 + nrm((DEPTH, G, P), 0.01)
    inp["ssm_a_im"] = math.pi * n_idx + nrm((DEPTH, G, P), 0.01)
    inp["ssm_log_dt"] = jax.random.uniform(next(keys), (DEPTH, G), F32, math.log(1e-3), math.log(1e-1))
    inp["ssm_b_re"] = nrm((DEPTH, G, P, H), (2 * H) ** -0.5)
    inp["ssm_b_im"] = nrm((DEPTH, G, P, H), (2 * H) ** -0.5)
    inp["ssm_c_re"] = nrm((DEPTH, G, H, P), (2 * P) ** -0.5)
    inp["ssm_c_im"] = nrm((DEPTH, G, H, P), (2 * P) ** -0.5)
    inp["ssm_d"] = nrm((DEPTH, D_SSM))
    inp["w_glu"] = nrm((DEPTH, D_SSM, D_SSM), D_SSM ** -0.5)
    inp["b_glu"] = nrm((DEPTH, D_SSM), 0.01)
    inp["lambda_q"] = nrm((DEPTH, 2, HEAD_DIM), 0.1)
    inp["lambda_k"] = nrm((DEPTH, 2, HEAD_DIM), 0.1)
    inp["g_subln"] = gain((DEPTH, V_DIM))
    inp["w_out"] = nrm((DEPTH, D_MIX, D_MODEL), D_MIX ** -0.5)
    inp["g_mem"] = gain((DEPTH, D_MODEL))
    inp["g_cross"] = gain((DEPTH, D_MODEL))
    inp["w_cq"] = nrm((DEPTH, D_MODEL, D_MODEL), D_MODEL ** -0.5)
    inp["w_ck"] = nrm((DEPTH, D_MODEL, D_MODEL), D_MODEL ** -0.5)
    inp["w_cv"] = nrm((DEPTH, D_MODEL, D_MODEL), D_MODEL ** -0.5)
    inp["w_co"] = nrm((DEPTH, D_MODEL, D_MODEL), D_MODEL ** -0.5)
    inp["g_ffn2"] = gain((DEPTH, D_MODEL))
    inp["w_ffn2_gu"] = nrm((DEPTH, D_MODEL, 2 * D_FF), D_MODEL ** -0.5)
    inp["w_ffn2_d"] = nrm((DEPTH, D_FF, D_MODEL), D_FF ** -0.5)
    inp["g_final"] = gain((D_MODEL,))
    return inp


def reference(x_prompt, x_sample, cache_attn_k, cache_attn_v, state_s5_re, state_s5_im,
              cache_mem_k, cache_mem_v, mem_prompt,
              g_ffn1, w_ffn1_gu, w_ffn1_d, g_mix, w_in,
              ssm_a_re, ssm_a_im, ssm_log_dt, ssm_b_re, ssm_b_im, ssm_c_re, ssm_c_im, ssm_d,
              w_glu, b_glu, lambda_q, lambda_k, g_subln, w_out,
              g_mem, g_cross, w_cq, w_ck, w_cv, w_co,
              g_ffn2, w_ffn2_gu, w_ffn2_d, g_final):
    xp, xs = x_prompt, x_sample
    kp_l, vp_l, rep_l, imp_l, mkp_l, mvp_l = [], [], [], [], [], []
    ks_l, vs_l, res_l, ims_l = [], [], [], []
    for l in range(DEPTH):
        lam_init = 0.8 - 0.6 * math.exp(-0.3 * l)
        lam = diff_lambda(lambda_q[l], lambda_k[l], lam_init)
        ssm = (ssm_a_re[l], ssm_a_im[l], ssm_log_dt[l], ssm_b_re[l], ssm_b_im[l],
               ssm_c_re[l], ssm_c_im[l], ssm_d[l], w_glu[l], b_glu[l])

        hp = ffn_half(xp, g_ffn1[l], w_ffn1_gu[l], w_ffn1_d[l])
        up, qp, kp, vp = mix_inputs(hp, g_mix[l], w_in[l])
        zero_state = jnp.zeros((xp.shape[0], N_SSM_GROUPS, SSM_STATE), F32)
        yp_ssm, rep, imp = s5_mixer(up, zero_state, zero_state, *ssm)
        yp_att = diff_attn_prompt(qp, kp, vp, lam, lam_init, g_subln[l])
        hp = hp + merge_groups(yp_ssm, yp_att, w_out[l])
        mkp, mvp = mem_kv(mem_prompt, g_mem[l], w_ck[l], w_cv[l])
        hp = hp + cross_attend(hp, g_cross[l], w_cq[l], w_co[l], mkp, mvp)
        xp = ffn_half(hp, g_ffn2[l], w_ffn2_gu[l], w_ffn2_d[l])

        hs = ffn_half(xs, g_ffn1[l], w_ffn1_gu[l], w_ffn1_d[l])
        us, qs, ks, vs = mix_inputs(hs, g_mix[l], w_in[l])
        ys_ssm, res, ims = s5_mixer(us, state_s5_re[l], state_s5_im[l], *ssm)
        ys_att = diff_attn_sample(qs, ks, vs, cache_attn_k[l], cache_attn_v[l], lam, lam_init, g_subln[l])
        hs = hs + merge_groups(ys_ssm, ys_att, w_out[l])
        hs = hs + cross_attend(hs, g_cross[l], w_cq[l], w_co[l], cache_mem_k[l], cache_mem_v[l])
        xs = ffn_half(hs, g_ffn2[l], w_ffn2_gu[l], w_ffn2_d[l])

        kp_l.append(kp); vp_l.append(vp); rep_l.append(rep); imp_l.append(imp)
        mkp_l.append(mkp); mvp_l.append(mvp)
        ks_l.append(ks); vs_l.append(vs); res_l.append(res); ims_l.append(ims)

    y_prompt = rmsnorm(xp, g_final)
    y_sample = rmsnorm(xs, g_final)
    return (y_prompt, y_sample,
            jnp.stack(kp_l), jnp.stack(vp_l), jnp.stack(rep_l), jnp.stack(imp_l),
            jnp.stack(mkp_l), jnp.stack(mvp_l),
            jnp.stack(ks_l), jnp.stack(vs_l), jnp.stack(res_l), jnp.stack(ims_l))
```

```python
import functools
import math

import jax
import jax.numpy as jnp
from jax import lax
from jax.experimental import pallas as pl
from jax.experimental.pallas import tpu as pltpu

F32 = jnp.float32
BF16 = jnp.bfloat16

D_MODEL = 1024
CHUNK = 64
D_SSM = 512
SSM_GROUP = 16
N_SSM_GROUPS = 32
SSM_STATE = 64
D_ATT = 512
N_HEADS = 4
HEAD_DIM = 64
V_DIM = 128
D_IN = D_SSM + 3 * D_ATT
N_MEM = 256
N_MEM_HEADS = 4
MEM_HEAD_DIM = 256
D_FF = 2816
EPS = 1e-6

N_STATE = N_SSM_GROUPS * SSM_STATE
SSM_BLOCKS = 2
BLK_CH = D_SSM // SSM_BLOCKS
BLK_ST = N_STATE // SSM_BLOCKS

V7X_VMEM_LIMIT = 56 * 1024 * 1024

_NT = (((1,), (1,)), ((), ()))


def _rmsnorm(x, g):
    return x * lax.rsqrt(jnp.mean(x * x, axis=-1, keepdims=True) + EPS) * g


def _dot(a, b):
    return jnp.dot(a, b, preferred_element_type=F32)


def _const_spec(shape):
    nd = len(shape)
    return pl.BlockSpec(shape, lambda *_: (0,) * nd, pipeline_mode=pl.Buffered(1))


def _swiglu_half(xn, wgu_ref, wd_ref, ff_chunks):
    cw = D_FF // ff_chunks
    acc = None
    for c in range(ff_chunks):
        gate = _dot(xn, wgu_ref[:, c * cw:(c + 1) * cw])
        up = _dot(xn, wgu_ref[:, D_FF + c * cw:D_FF + (c + 1) * cw])
        act = (gate * jax.nn.sigmoid(gate) * up).astype(BF16)
        part = _dot(act, wd_ref[c * cw:(c + 1) * cw, :])
        acc = part if acc is None else acc + part
    return acc


def _ffn_mix_kernel(x_ref, g1_ref, wgu_ref, wd_ref, gm_ref, win_ref,
                    h_ref, u_ref, qb_ref, k_ref, v_ref, kb_ref, vb_ref, *, nb, rpb, ff_chunks):
    x = x_ref[...]
    xn = _rmsnorm(x, g1_ref[...]).astype(BF16)
    h = x + 0.5 * _swiglu_half(xn, wgu_ref, wd_ref, ff_chunks)
    h_ref[...] = h
    hn = _rmsnorm(h, gm_ref[...]).astype(BF16)
    z = _dot(hn, win_ref[...])
    for lb in range(nb):
        u_ref[:, lb * D_SSM:(lb + 1) * D_SSM] = z[lb * rpb:(lb + 1) * rpb, :D_SSM]
    q = z[:, D_SSM:D_SSM + D_ATT]
    k = z[:, D_SSM + D_ATT:D_SSM + 2 * D_ATT]
    v = z[:, D_SSM + 2 * D_ATT:]
    qb_ref[...] = (q * (HEAD_DIM ** -0.5)).astype(BF16)
    k_ref[...] = k
    v_ref[...] = v
    kb_ref[...] = k.astype(BF16)
    vb_ref[...] = v.astype(BF16)


def _ffn_mix(x2d, g1, wgu, wd, gm, win, *, n_streams, seq, tm, ff_chunks=2):
    T = x2d.shape[0]
    if seq >= tm:
        nb, rpb = 1, tm
        nt = seq // tm
        u_map = lambda i: (i % nt, i // nt)
    else:
        nb, rpb = tm // seq, seq
        u_map = lambda i: (0, i)
    tok = lambda w: pl.BlockSpec((tm, w), lambda i: (i, 0))
    out_shape = (
        jax.ShapeDtypeStruct((T, D_MODEL), F32),
        jax.ShapeDtypeStruct((seq, n_streams * D_SSM), F32),
        jax.ShapeDtypeStruct((T, D_ATT), BF16),
        jax.ShapeDtypeStruct((T, D_ATT), F32),
        jax.ShapeDtypeStruct((T, D_ATT), F32),
        jax.ShapeDtypeStruct((T, D_ATT), BF16),
        jax.ShapeDtypeStruct((T, D_ATT), BF16),
    )
    return pl.pallas_call(
        functools.partial(_ffn_mix_kernel, nb=nb, rpb=rpb, ff_chunks=ff_chunks),
        grid=(T // tm,),
        in_specs=[tok(D_MODEL), _const_spec((1, D_MODEL)), _const_spec(wgu.shape), _const_spec(wd.shape),
                  _const_spec((1, D_MODEL)), _const_spec(win.shape)],
        out_specs=(tok(D_MODEL), pl.BlockSpec((rpb, nb * D_SSM), u_map),
                   tok(D_ATT), tok(D_ATT), tok(D_ATT), tok(D_ATT), tok(D_ATT)),
        out_shape=out_shape,
        compiler_params=pltpu.CompilerParams(dimension_semantics=("arbitrary",),
                                             vmem_limit_bytes=V7X_VMEM_LIMIT),
        name="ffn_mix",
    )(x2d, g1, wgu, wd, gm, win)


def _s5_prep_kernel(are_ref, aim_ref, ldt_ref, bre_ref, bim_ref, abr_ref, abi_ref, bbr_ref, bbi_ref):
    lre = are_ref[...]
    lim = aim_ref[...]
    dt = jnp.exp(ldt_ref[...])
    mag = jnp.exp(lre * dt)
    ar = mag * jnp.cos(lim * dt)
    ai = mag * jnp.sin(lim * dt)
    abr_ref[...] = ar
    abi_ref[...] = ai
    den = lre * lre + lim * lim
    xr = ar - 1.0
    cr = (xr * lre + ai * lim) / den
    ci = (ai * lre - xr * lim) / den
    br = bre_ref[...]
    bi = bim_ref[...]
    bbr_ref[...] = cr * br - ci * bi
    bbi_ref[...] = cr * bi + ci * br


def _s5_prep(a_re, a_im, log_dt, b_re, b_im):
    G, P, H = N_SSM_GROUPS, SSM_STATE, SSM_GROUP
    per_row = lambda a: jnp.repeat(a, H, axis=0)
    b_re_t = jnp.swapaxes(b_re, 1, 2).reshape(G * H, P)
    b_im_t = jnp.swapaxes(b_im, 1, 2).reshape(G * H, P)
    o = jax.ShapeDtypeStruct((G * H, P), F32)
    abr, abi, bbr, bbi = pl.pallas_call(
        _s5_prep_kernel,
        out_shape=(o, o, o, o),
        name="s5_prep",
    )(per_row(a_re), per_row(a_im), per_row(log_dt.reshape(G, 1)), b_re_t, b_im_t)
    return (abr.reshape(G, H, P)[:, 0], abi.reshape(G, H, P)[:, 0], bbr.reshape(G, H, P), bbi.reshape(G, H, P))


def _block_diag_in(w_ghp):
    gpb = N_SSM_GROUPS // SSM_BLOCKS
    w = w_ghp.reshape(SSM_BLOCKS, gpb, SSM_GROUP, SSM_STATE)
    eye = jnp.eye(gpb, dtype=w.dtype)
    return jnp.einsum('jghp,gk->jghkp', w, eye).reshape(SSM_BLOCKS, BLK_CH, BLK_ST)


def _block_diag_out(w_ghp):
    gpb = N_SSM_GROUPS // SSM_BLOCKS
    w = w_ghp.reshape(SSM_BLOCKS, gpb, SSM_GROUP, SSM_STATE)
    eye = jnp.eye(gpb, dtype=w.dtype)
    return jnp.einsum('jghp,gk->jgpkh', w, eye).reshape(SSM_BLOCKS, BLK_ST, BLK_CH)


def _gelu_tanh(x):
    return 0.5 * x * (1.0 + jnp.tanh(math.sqrt(2.0 / math.pi) * (x + 0.044715 * (x * x * x))))


def _s5_kernel(u_ref, h0_ref, ab_ref, bbig_ref, cbig_ref, d_ref, wglu_ref, bglu_ref,
               y_ref, hl_ref, hst, abr, abi, bu_sc, *, nb, tc, slab):
    c = pl.program_id(0)

    @pl.when(c == 0)
    def _():
        hst[...] = h0_ref[...]
        abr[...] = jnp.broadcast_to(ab_ref[0:1, :], (nb, N_STATE))
        abi[...] = jnp.broadcast_to(ab_ref[1:2, :], (nb, N_STATE))

    u = u_ref[...]
    ub = u.astype(BF16)
    ys = []
    for j in range(SSM_BLOCKS):
        bu_sc[...] = _dot(ub[:, BLK_CH * j:BLK_CH * (j + 1)], bbig_ref[j])
        for s in range(BLK_ST // slab):
            cr = s * slab
            ci = BLK_ST + s * slab
            ar = abr[:, BLK_ST * j + cr:BLK_ST * j + cr + slab]
            ai = abi[:, BLK_ST * j + cr:BLK_ST * j + cr + slab]
            sr = 2 * BLK_ST * j + cr
            si = 2 * BLK_ST * j + ci

            def step(t, carry, ar=ar, ai=ai, cr=cr, ci=ci):
                hr, hi = carry
                r0 = pl.multiple_of(t * nb, nb)
                br = bu_sc[pl.ds(r0, nb), cr:cr + slab]
                bi = bu_sc[pl.ds(r0, nb), ci:ci + slab]
                nr = ar * hr - ai * hi + br
                ni = ar * hi + ai * hr + bi
                bu_sc[pl.ds(r0, nb), cr:cr + slab] = nr
                bu_sc[pl.ds(r0, nb), ci:ci + slab] = ni
                return nr, ni

            hr, hi = lax.fori_loop(0, tc, step, (hst[:, sr:sr + slab], hst[:, si:si + slab]), unroll=4)
            hst[:, sr:sr + slab] = hr
            hst[:, si:si + slab] = hi
        ys.append(_dot(bu_sc[...].astype(BF16), cbig_ref[j]))
    y = jnp.concatenate(ys, axis=1) + d_ref[...] * u
    y = _gelu_tanh(y)
    gate = _dot(y.astype(BF16), wglu_ref[...]) + bglu_ref[...]
    y_ref[...] = (y * jax.nn.sigmoid(gate)).astype(BF16)

    @pl.when(c == pl.num_programs(0) - 1)
    def _():
        hl_ref[...] = hst[...]


def _s5(u_tm, h0, ab, bbig, cbig, d, wglu, bglu, *, nb, seq, tc, slab):
    rows = tc * nb
    return pl.pallas_call(
        functools.partial(_s5_kernel, nb=nb, tc=tc, slab=slab),
        grid=(seq // tc,),
        in_specs=[pl.BlockSpec((rows, D_SSM), lambda c: (c, 0)),
                  _const_spec((nb, 2 * N_STATE)), _const_spec((2, N_STATE)),
                  _const_spec(bbig.shape), _const_spec(cbig.shape),
                  _const_spec((1, D_SSM)), _const_spec((D_SSM, D_SSM)), _const_spec((1, D_SSM))],
        out_specs=(pl.BlockSpec((rows, D_SSM), lambda c: (c, 0)),
                   pl.BlockSpec((nb, 2 * N_STATE), lambda c: (0, 0))),
        out_shape=(jax.ShapeDtypeStruct((seq * nb, D_SSM), BF16),
                   jax.ShapeDtypeStruct((nb, 2 * N_STATE), F32)),
        scratch_shapes=[pltpu.VMEM((nb, 2 * N_STATE), F32), pltpu.VMEM((nb, N_STATE), F32),
                        pltpu.VMEM((nb, N_STATE), F32), pltpu.VMEM((rows, 2 * BLK_ST), F32)],
        compiler_params=pltpu.CompilerParams(dimension_semantics=("arbitrary",),
                                             vmem_limit_bytes=V7X_VMEM_LIMIT),
        name="s5_scan",
    )(u_tm, h0, ab, bbig, cbig, d, wglu, bglu)


def _state_to_cols(s_re, s_im):
    nb = s_re.shape[0]
    r = s_re.reshape(nb, SSM_BLOCKS, 1, BLK_ST)
    i = s_im.reshape(nb, SSM_BLOCKS, 1, BLK_ST)
    return jnp.concatenate([r, i], axis=2).reshape(nb, 2 * N_STATE)


def _cols_to_state(h):
    nb = h.shape[0]
    h4 = h.reshape(nb, SSM_BLOCKS, 2, BLK_ST)
    return (h4[:, :, 0].reshape(nb, N_SSM_GROUPS, SSM_STATE), h4[:, :, 1].reshape(nb, N_SSM_GROUPS, SSM_STATE))


def _two_map_queries(q):
    lane = lax.broadcasted_iota(jnp.int32, q.shape, 1)
    lo = lane < HEAD_DIM
    zero = jnp.zeros_like(q)
    return jnp.concatenate([jnp.where(lo, q, zero), jnp.where(lo, zero, q)], axis=0)


def _diff_lambda(lq_ref, lk_ref, lam_init):
    p = lq_ref[...] * lk_ref[...]
    s = jnp.sum(p, axis=1, keepdims=True)
    e = jnp.exp(s)
    return e[0:1, :] - e[1:2, :] + lam_init


def _online_softmax_step(s, vb, m_sc, l_sc, acc_sc):
    m_old = m_sc[...]
    m_new = jnp.maximum(m_old, jnp.max(s, axis=1, keepdims=True))
    alpha = jnp.exp(m_old - m_new)
    p = jnp.exp(s - m_new)
    l_sc[...] = alpha * l_sc[...] + jnp.sum(p, axis=1, keepdims=True)
    acc_sc[...] = alpha * acc_sc[...] + _dot(p.astype(BF16), vb)
    m_sc[...] = m_new


def _diff_finish(acc, l, n, lam, g, lam_init):
    o = acc[:n] / l[:n] - lam * (acc[n:] / l[n:])
    return _rmsnorm(o, g) * (1.0 - lam_init)


def _attn_prompt_kernel(slopes_ref, q_ref, k_ref, v_ref, lq_ref, lk_ref, g_ref, o_ref,
                        m_sc, l_sc, acc_sc, *, bq, lam_init):
    h = pl.program_id(1)
    i = pl.program_id(2)
    slope = slopes_ref[h]
    qq = _two_map_queries(q_ref[0])
    m_sc[...] = jnp.full(m_sc.shape, -jnp.inf, F32)
    l_sc[...] = jnp.zeros(l_sc.shape, F32)
    acc_sc[...] = jnp.zeros(acc_sc.shape, F32)
    row = lax.broadcasted_iota(jnp.int32, (2 * bq, 1), 0)
    qpos = i * bq + jnp.where(row >= bq, row - bq, row)

    def step(j, masked):
        start = pl.multiple_of(j * bq, bq)
        kb = k_ref[0, pl.ds(start, bq), :]
        vb = v_ref[0, pl.ds(start, bq), :]
        s = lax.dot_general(qq, kb, _NT, preferred_element_type=F32)
        kpos = j * bq + lax.broadcasted_iota(jnp.int32, (1, bq), 1)
        s = s - slope * jnp.abs(qpos - kpos).astype(F32)
        if masked:
            s = jnp.where((kpos // CHUNK) <= (qpos // CHUNK), s, -jnp.inf)
        _online_softmax_step(s, vb, m_sc, l_sc, acc_sc)

    def body(j, carry):
        step(j, False)
        return carry

    lax.fori_loop(0, i, body, 0)
    step(i, True)
    lam = _diff_lambda(lq_ref, lk_ref, lam_init)
    o_ref[0] = _diff_finish(acc_sc[...], l_sc[...], bq, lam, g_ref[...], lam_init).astype(o_ref.dtype)


def _attn_prompt(qb, kb, vb, slopes, lq, lk, g, *, lam_init, bq=256):
    B, L, _ = qb.shape
    return pl.pallas_call(
        functools.partial(_attn_prompt_kernel, bq=bq, lam_init=lam_init),
        grid=(B, N_HEADS, L // bq),
        in_specs=[pl.BlockSpec(memory_space=pltpu.SMEM),
                  pl.BlockSpec((1, bq, V_DIM), lambda b, h, i: (b, i, h)),
                  pl.BlockSpec((1, L, V_DIM), lambda b, h, i: (b, 0, h)),
                  pl.BlockSpec((1, L, V_DIM), lambda b, h, i: (b, 0, h)),
                  pl.BlockSpec((2, HEAD_DIM), lambda b, h, i: (0, 0)),
                  pl.BlockSpec((2, HEAD_DIM), lambda b, h, i: (0, 0)),
                  pl.BlockSpec((1, V_DIM), lambda b, h, i: (0, 0))],
        out_specs=pl.BlockSpec((1, bq, V_DIM), lambda b, h, i: (b, i, h)),
        out_shape=jax.ShapeDtypeStruct((B, L, D_ATT), BF16),
        scratch_shapes=[pltpu.VMEM((2 * bq, 1), F32), pltpu.VMEM((2 * bq, 1), F32),
                        pltpu.VMEM((2 * bq, V_DIM), F32)],
        compiler_params=pltpu.CompilerParams(dimension_semantics=("arbitrary", "arbitrary", "arbitrary")),
        name="attn_prompt",
    )(slopes, qb, kb, vb, lq, lk, g)


def _attn_sample_kernel(q_ref, kc_ref, vc_ref, kn_ref, vn_ref, lq_ref, lk_ref, g_ref, o_ref,
                        qq_sc, m_sc, l_sc, acc_sc, *, nq, bk, past, lam_init):
    j = pl.program_id(1)

    @pl.when(j == 0)
    def _():
        for h in range(N_HEADS):
            qq_sc[h] = _two_map_queries(q_ref[0, :, h * V_DIM:(h + 1) * V_DIM])
        m_sc[...] = jnp.full(m_sc.shape, -jnp.inf, F32)
        l_sc[...] = jnp.zeros(l_sc.shape, F32)
        acc_sc[...] = jnp.zeros(acc_sc.shape, F32)

    row = lax.broadcasted_iota(jnp.int32, (2 * nq, 1), 0)
    qpos = past + jnp.where(row >= nq, row - nq, row)

    def block(k_blk_ref, v_blk_ref, kpos0, n):
        kpos = kpos0 + lax.broadcasted_iota(jnp.int32, (1, n), 1)
        dist = jnp.abs(qpos - kpos).astype(F32)
        for h in range(N_HEADS):
            slope = 2.0 ** (-8.0 * (h + 1) / N_HEADS)
            kb = k_blk_ref[0, :, h * V_DIM:(h + 1) * V_DIM].astype(BF16)
            vb = v_blk_ref[0, :, h * V_DIM:(h + 1) * V_DIM].astype(BF16)
            s = lax.dot_general(qq_sc[h], kb, _NT, preferred_element_type=F32) - slope * dist
            _online_softmax_step(s, vb, m_sc.at[h], l_sc.at[h], acc_sc.at[h])

    block(kc_ref, vc_ref, j * bk, bk)

    @pl.when(j == pl.num_programs(1) - 1)
    def _():
        block(kn_ref, vn_ref, past, nq)
        lam = _diff_lambda(lq_ref, lk_ref, lam_init)
        for h in range(N_HEADS):
            o = _diff_finish(acc_sc[h], l_sc[h], nq, lam, g_ref[...], lam_init)
            o_ref[0, :, h * V_DIM:(h + 1) * V_DIM] = o.astype(o_ref.dtype)


def _attn_sample(qb, k_cache, v_cache, k_new, v_new, lq, lk, g, *, lam_init, bk=1024):
    B, nq, _ = qb.shape
    past = k_cache.shape[1]
    new_spec = pl.BlockSpec((1, nq, D_ATT), lambda b, j: (b, 0, 0))
    cache_spec = pl.BlockSpec((1, bk, D_ATT), lambda b, j: (b, j, 0))
    small = lambda shape: pl.BlockSpec(shape, lambda b, j: (0, 0))
    return pl.pallas_call(
        functools.partial(_attn_sample_kernel, nq=nq, bk=bk, past=past, lam_init=lam_init),
        grid=(B, past // bk),
        in_specs=[new_spec, cache_spec, cache_spec, new_spec, new_spec,
                  small((2, HEAD_DIM)), small((2, HEAD_DIM)), small((1, V_DIM))],
        out_specs=new_spec,
        out_shape=jax.ShapeDtypeStruct((B, nq, D_ATT), BF16),
        scratch_shapes=[pltpu.VMEM((N_HEADS, 2 * nq, V_DIM), BF16), pltpu.VMEM((N_HEADS, 2 * nq, 1), F32),
                        pltpu.VMEM((N_HEADS, 2 * nq, 1), F32), pltpu.VMEM((N_HEADS, 2 * nq, V_DIM), F32)],
        compiler_params=pltpu.CompilerParams(dimension_semantics=("arbitrary", "arbitrary")),
        name="attn_sample",
    )(qb, k_cache, v_cache, k_new, v_new, lq, lk, g)


def _mem_kv_kernel(m_ref, g_ref, wck_ref, wcv_ref, mk_ref, mv_ref, mkb_ref, mvb_ref):
    mn = _rmsnorm(m_ref[...], g_ref[...]).astype(BF16)
    k = _dot(mn, wck_ref[...])
    v = _dot(mn, wcv_ref[...])
    mk_ref[...] = k
    mv_ref[...] = v
    mkb_ref[...] = k.astype(BF16)
    mvb_ref[...] = v.astype(BF16)


def _mem_kv(mem2d, g, wck, wcv, *, tm=512):
    T = mem2d.shape[0]
    tok = pl.BlockSpec((tm, D_MODEL), lambda i: (i, 0))
    f = jax.ShapeDtypeStruct((T, D_MODEL), F32)
    b = jax.ShapeDtypeStruct((T, D_MODEL), BF16)
    return pl.pallas_call(
        _mem_kv_kernel,
        grid=(T // tm,),
        in_specs=[tok, _const_spec((1, D_MODEL)), _const_spec(wck.shape), _const_spec(wcv.shape)],
        out_specs=(tok, tok, tok, tok),
        out_shape=(f, f, b, b),
        compiler_params=pltpu.CompilerParams(dimension_semantics=("arbitrary",),
                                             vmem_limit_bytes=V7X_VMEM_LIMIT),
        name="mem_kv",
    )(mem2d, g, wck, wcv)


def _tail_kernel(h_ref, ys_ref, ya_ref, mk_ref, mv_ref, wout_ref, gc_ref, wcq_ref, wco_ref,
                 g2_ref, wgu_ref, wd_ref, gf_ref, y_ref, *, nb, rpb, ff_chunks, final_norm):
    if nb == 1:
        ys = ys_ref[...]
    else:
        ys = jnp.concatenate([ys_ref[:, lb * D_SSM:(lb + 1) * D_SSM] for lb in range(nb)], axis=0)
    h = h_ref[...] + _dot(ys, wout_ref[:D_SSM, :]) + _dot(ya_ref[...], wout_ref[D_SSM:, :])

    qn = _rmsnorm(h, gc_ref[...]).astype(BF16)
    q = (_dot(qn, wcq_ref[...]) * (MEM_HEAD_DIM ** -0.5)).astype(BF16)
    rows = []
    for lb in range(nb):
        heads = []
        for hh in range(N_MEM_HEADS):
            cols = slice(hh * MEM_HEAD_DIM, (hh + 1) * MEM_HEAD_DIM)
            qh = q[lb * rpb:(lb + 1) * rpb, cols]
            s = lax.dot_general(qh, mk_ref[lb, :, cols], _NT, preferred_element_type=F32)
            p = jnp.exp(s - jnp.max(s, axis=1, keepdims=True))
            o = _dot(p.astype(BF16), mv_ref[lb, :, cols]) / jnp.sum(p, axis=1, keepdims=True)
            heads.append(o)
        rows.append(jnp.concatenate(heads, axis=1))
    o = rows[0] if nb == 1 else jnp.concatenate(rows, axis=0)
    h = h + _dot(o.astype(BF16), wco_ref[...])

    xn = _rmsnorm(h, g2_ref[...]).astype(BF16)
    x2 = h + 0.5 * _swiglu_half(xn, wgu_ref, wd_ref, ff_chunks)
    y_ref[...] = _rmsnorm(x2, gf_ref[...]) if final_norm else x2


def _tail(h2d, ys_tm, ya2d, mkb, mvb, wout, gc, wcq, wco, g2, wgu, wd, gf, *, n_streams, seq, tm, final_norm,
          ff_chunks=2):
    T = h2d.shape[0]
    if seq >= tm:
        nb, rpb = 1, tm
        nt = seq // tm
        ys_map = lambda i: (i % nt, i // nt)
        mem_map = lambda i: (i // nt, 0, 0)
    else:
        nb, rpb = tm // seq, seq
        ys_map = lambda i: (0, i)
        mem_map = lambda i: (i, 0, 0)
    tok = lambda w: pl.BlockSpec((tm, w), lambda i: (i, 0))
    mem_spec = pl.BlockSpec((nb, N_MEM, D_MODEL), mem_map)
    return pl.pallas_call(
        functools.partial(_tail_kernel, nb=nb, rpb=rpb, ff_chunks=ff_chunks, final_norm=final_norm),
        grid=(T // tm,),
        in_specs=[tok(D_MODEL), pl.BlockSpec((rpb, nb * D_SSM), ys_map), tok(D_ATT), mem_spec, mem_spec,
                  _const_spec(wout.shape), _const_spec((1, D_MODEL)), _const_spec(wcq.shape),
                  _const_spec(wco.shape), _const_spec((1, D_MODEL)), _const_spec(wgu.shape),
                  _const_spec(wd.shape), _const_spec((1, D_MODEL))],
        out_specs=tok(D_MODEL),
        out_shape=jax.ShapeDtypeStruct((T, D_MODEL), F32),
        compiler_params=pltpu.CompilerParams(dimension_semantics=("arbitrary",),
                                             vmem_limit_bytes=V7X_VMEM_LIMIT),
        name="tail",
    )(h2d, ys_tm, ya2d, mkb, mvb, wout, gc, wcq, wco, g2, wgu, wd, gf)


def kernel(x_prompt, x_sample, cache_attn_k, cache_attn_v, state_s5_re, state_s5_im, cache_mem_k, cache_mem_v, mem_prompt, g_ffn1, w_ffn1_gu, w_ffn1_d, g_mix, w_in, ssm_a_re, ssm_a_im, ssm_log_dt, ssm_b_re, ssm_b_im, ssm_c_re, ssm_c_im, ssm_d, w_glu, b_glu, lambda_q, lambda_k, g_subln, w_out, g_mem, g_cross, w_cq, w_ck, w_cv, w_co, g_ffn2, w_ffn2_gu, w_ffn2_d, g_final):
    depth = g_ffn1.shape[0]
    B, L, D = x_prompt.shape
    SB, SL, _ = x_sample.shape
    past = cache_attn_k.shape[2]
    row = lambda a: a.reshape(1, -1)
    slopes = jnp.asarray([2.0 ** (-8.0 * (i + 1) / N_HEADS) for i in range(N_HEADS)], F32)

    xp = x_prompt.reshape(B * L, D)
    xs = x_sample.reshape(SB * SL, D)
    outs = [[] for _ in range(10)]
    for l in range(depth):
        lam_init = 0.8 - 0.6 * math.exp(-0.3 * l)
        wgu1, wd1, win = w_ffn1_gu[l].astype(BF16), w_ffn1_d[l].astype(BF16), w_in[l].astype(BF16)
        wgu2, wd2 = w_ffn2_gu[l].astype(BF16), w_ffn2_d[l].astype(BF16)
        wout, wcq, wco = w_out[l].astype(BF16), w_cq[l].astype(BF16), w_co[l].astype(BF16)
        wck, wcv, wglu = w_ck[l].astype(BF16), w_cv[l].astype(BF16), w_glu[l].astype(BF16)

        abr, abi, bbr, bbi = _s5_prep(ssm_a_re[l], ssm_a_im[l], ssm_log_dt[l], ssm_b_re[l], ssm_b_im[l])
        ab = jnp.stack([abr.reshape(N_STATE), abi.reshape(N_STATE)])
        bbig = jnp.concatenate([_block_diag_in(bbr), _block_diag_in(bbi)], axis=2).astype(BF16)
        cbig = jnp.concatenate([_block_diag_out(ssm_c_re[l]), -_block_diag_out(ssm_c_im[l])], axis=1).astype(BF16)
        s5_w = (ab, bbig, cbig, row(ssm_d[l]), wglu, row(b_glu[l]))
        att_w = (lambda_q[l], lambda_k[l], row(g_subln[l]))
        tail_w = (wout, row(g_cross[l]), wcq, wco, row(g_ffn2[l]), wgu2, wd2, row(g_final))

        hp, up, qp, kp, vp, kpb, vpb = _ffn_mix(xp, row(g_ffn1[l]), wgu1, wd1, row(g_mix[l]), win,
                                                 n_streams=B, seq=L, tm=512)
        zeros = jnp.zeros((B, 2 * N_STATE), F32)
        yp_ssm, hlp = _s5(up.reshape(L * B, D_SSM), zeros, *s5_w, nb=B, seq=L, tc=64, slab=512)
        yp_att = _attn_prompt(qp.reshape(B, L, D_ATT), kpb.reshape(B, L, D_ATT), vpb.reshape(B, L, D_ATT),
                              slopes, *att_w, lam_init=lam_init)
        mkp, mvp, mkpb, mvpb = _mem_kv(mem_prompt.reshape(B * N_MEM, D), row(g_mem[l]), wck, wcv)
        xp = _tail(hp, yp_ssm.reshape(L, B * D_SSM), yp_att.reshape(B * L, D_ATT),
                   mkpb.reshape(B, N_MEM, D), mvpb.reshape(B, N_MEM, D), *tail_w,
                   n_streams=B, seq=L, tm=512, final_norm=(l == depth - 1))

        hs, us, qs, ks, vs, _, _ = _ffn_mix(xs, row(g_ffn1[l]), wgu1, wd1, row(g_mix[l]), win,
                                            n_streams=SB, seq=SL, tm=512)
        h0 = _state_to_cols(state_s5_re[l], state_s5_im[l])
        ys_ssm, hls = _s5(us.reshape(SL * SB, D_SSM), h0, *s5_w, nb=SB, seq=SL, tc=SL, slab=256)
        ys_att = _attn_sample(qs.reshape(SB, SL, D_ATT), cache_attn_k[l].reshape(SB, past, D_ATT),
                              cache_attn_v[l].reshape(SB, past, D_ATT), ks.reshape(SB, SL, D_ATT),
                              vs.reshape(SB, SL, D_ATT), *att_w, lam_init=lam_init)
        xs = _tail(hs, ys_ssm.reshape(SL, SB * D_SSM), ys_att.reshape(SB * SL, D_ATT),
                   cache_mem_k[l].reshape(SB, N_MEM, D).astype(BF16),
                   cache_mem_v[l].reshape(SB, N_MEM, D).astype(BF16), *tail_w,
                   n_streams=SB, seq=SL, tm=256, final_norm=(l == depth - 1))

        rep, imp = _cols_to_state(hlp)
        res, ims = _cols_to_state(hls)
        layer = (kp.reshape(B, L, N_HEADS, V_DIM), vp.reshape(B, L, N_HEADS, V_DIM), rep, imp,
                 mkp.reshape(B, N_MEM, N_MEM_HEADS, MEM_HEAD_DIM), mvp.reshape(B, N_MEM, N_MEM_HEADS, MEM_HEAD_DIM),
                 ks.reshape(SB, SL, N_HEADS, V_DIM), vs.reshape(SB, SL, N_HEADS, V_DIM), res, ims)
        for o, a in zip(outs, layer):
            o.append(a)

    return (xp.reshape(B, L, D), xs.reshape(SB, SL, D)) + tuple(jnp.stack(o) for o in outs)
```

```python
import functools
import math

import jax
import jax.numpy as jnp
from jax import lax
from jax.experimental import pallas as pl
from jax.experimental.pallas import tpu as pltpu

F32 = jnp.float32
BF16 = jnp.bfloat16

D_MODEL = 1024
CHUNK = 64
D_SSM = 512
SSM_GROUP = 16
N_SSM_GROUPS = 32
SSM_STATE = 64
D_ATT = 512
N_HEADS = 4
HEAD_DIM = 64
V_DIM = 128
D_IN = D_SSM + 3 * D_ATT
N_MEM = 256
N_MEM_HEADS = 4
MEM_HEAD_DIM = 256
D_FF = 2816
EPS = 1e-6
LOG2E = math.log2(math.e)

N_STATE = N_SSM_GROUPS * SSM_STATE
SSM_BLOCKS = 2
BLK_CH = D_SSM // SSM_BLOCKS
BLK_ST = N_STATE // SSM_BLOCKS

LANES = 128
V7X_VMEM_LIMIT = 56 * 1024 * 1024

_NT = (((1,), (1,)), ((), ()))
_TN = (((0,), (0,)), ((), ()))


def _rmsnorm(x, g):
    return x * lax.rsqrt(jnp.mean(x * x, axis=-1, keepdims=True) + EPS) * g


def _dot(a, b):
    return jnp.dot(a, b, preferred_element_type=F32)


def _const_spec(shape):
    nd = len(shape)
    return pl.BlockSpec(shape, lambda *_: (0,) * nd, pipeline_mode=pl.Buffered(1))


def _swiglu_half(xn, wgu_ref, wd_ref, ff_chunks):
    cw = D_FF // ff_chunks
    acc = None
    for c in range(ff_chunks):
        gate = _dot(xn, wgu_ref[:, c * cw:(c + 1) * cw])
        up = _dot(xn, wgu_ref[:, D_FF + c * cw:D_FF + (c + 1) * cw])
        act = (gate * jax.nn.sigmoid(gate) * up).astype(BF16)
        part = _dot(act, wd_ref[c * cw:(c + 1) * cw, :])
        acc = part if acc is None else acc + part
    return acc


def _store_heads(ref, x, n_heads, width):
    for hh in range(n_heads):
        ref[:, hh, :] = x[:, hh * width:(hh + 1) * width]


def _ffn_mix_kernel(x_ref, g1_ref, wgu_ref, wd_ref, gm_ref, win_ref,
                    h_ref, u_ref, qb_ref, k_ref, v_ref, *kvb_refs, nb, rpb, ff_chunks):
    x = x_ref[...]
    xn = _rmsnorm(x, g1_ref[...]).astype(BF16)
    h = x + 0.5 * _swiglu_half(xn, wgu_ref, wd_ref, ff_chunks)
    h_ref[...] = h
    hn = _rmsnorm(h, gm_ref[...]).astype(BF16)
    z = _dot(hn, win_ref[...])
    for lb in range(nb):
        u_ref[:, lb * D_SSM:(lb + 1) * D_SSM] = z[lb * rpb:(lb + 1) * rpb, :D_SSM]
    q = z[:, D_SSM:D_SSM + D_ATT]
    k = z[:, D_SSM + D_ATT:D_SSM + 2 * D_ATT]
    v = z[:, D_SSM + 2 * D_ATT:]
    qb_ref[...] = (q * (HEAD_DIM ** -0.5 * LOG2E)).astype(BF16)
    _store_heads(k_ref, k, N_HEADS, V_DIM)
    _store_heads(v_ref, v, N_HEADS, V_DIM)
    if kvb_refs:
        kb_ref, vb_ref = kvb_refs
        kb_ref[...] = k.astype(BF16)
        vb_ref[...] = v.astype(BF16)


def _ffn_mix(x2d, g1, wgu, wd, gm, win, *, n_streams, seq, tm, bf16_kv, ff_chunks=2):
    T = x2d.shape[0]
    if seq >= tm:
        nb, rpb = 1, tm
        nt = seq // tm
        u_map = lambda i: (i % nt, i // nt)
    else:
        nb, rpb = tm // seq, seq
        u_map = lambda i: (0, i)
    tok = lambda w: pl.BlockSpec((tm, w), lambda i: (i, 0))
    heads = pl.BlockSpec((tm, N_HEADS, V_DIM), lambda i: (i, 0, 0))
    out_shape = [
        jax.ShapeDtypeStruct((T, D_MODEL), F32),
        jax.ShapeDtypeStruct((seq, n_streams * D_SSM), F32),
        jax.ShapeDtypeStruct((T, D_ATT), BF16),
        jax.ShapeDtypeStruct((T, N_HEADS, V_DIM), F32),
        jax.ShapeDtypeStruct((T, N_HEADS, V_DIM), F32),
    ]
    out_specs = [tok(D_MODEL), pl.BlockSpec((rpb, nb * D_SSM), u_map), tok(D_ATT), heads, heads]
    if bf16_kv:
        out_shape += [jax.ShapeDtypeStruct((T, D_ATT), BF16)] * 2
        out_specs += [tok(D_ATT)] * 2
    return pl.pallas_call(
        functools.partial(_ffn_mix_kernel, nb=nb, rpb=rpb, ff_chunks=ff_chunks),
        grid=(T // tm,),
        in_specs=[tok(D_MODEL), _const_spec((1, D_MODEL)), _const_spec(wgu.shape), _const_spec(wd.shape),
                  _const_spec((1, D_MODEL)), _const_spec(win.shape)],
        out_specs=tuple(out_specs),
        out_shape=tuple(out_shape),
        compiler_params=pltpu.CompilerParams(dimension_semantics=("arbitrary",),
                                             vmem_limit_bytes=V7X_VMEM_LIMIT),
        name="ffn_mix",
    )(x2d, g1, wgu, wd, gm, win)


def _s5_prep_kernel(are_ref, aim_ref, ldt_ref, bre_ref, bim_ref, abr_ref, abi_ref, bbr_ref, bbi_ref):
    lre = are_ref[...]
    lim = aim_ref[...]
    dt = jnp.exp(ldt_ref[...])
    mag = jnp.exp(lre * dt)
    ar = mag * jnp.cos(lim * dt)
    ai = mag * jnp.sin(lim * dt)
    abr_ref[...] = ar
    abi_ref[...] = ai
    den = lre * lre + lim * lim
    xr = ar - 1.0
    cr = (xr * lre + ai * lim) / den
    ci = (ai * lre - xr * lim) / den
    br = bre_ref[...]
    bi = bim_ref[...]
    bbr_ref[...] = cr * br - ci * bi
    bbi_ref[...] = cr * bi + ci * br


def _s5_prep(a_re, a_im, log_dt, b_re, b_im):
    G, P, H = N_SSM_GROUPS, SSM_STATE, SSM_GROUP
    per_row = lambda a: jnp.repeat(a, H, axis=0)
    b_re_t = jnp.swapaxes(b_re, 1, 2).reshape(G * H, P)
    b_im_t = jnp.swapaxes(b_im, 1, 2).reshape(G * H, P)
    o = jax.ShapeDtypeStruct((G * H, P), F32)
    abr, abi, bbr, bbi = pl.pallas_call(
        _s5_prep_kernel,
        out_shape=(o, o, o, o),
        name="s5_prep",
    )(per_row(a_re), per_row(a_im), per_row(log_dt.reshape(G, 1)), b_re_t, b_im_t)
    return (abr.reshape(G, H, P)[:, 0], abi.reshape(G, H, P)[:, 0], bbr.reshape(G, H, P), bbi.reshape(G, H, P))


def _block_diag_in(w_ghp):
    gpb = N_SSM_GROUPS // SSM_BLOCKS
    w = w_ghp.reshape(SSM_BLOCKS, gpb, SSM_GROUP, SSM_STATE)
    eye = jnp.eye(gpb, dtype=w.dtype)
    return jnp.einsum('jghp,gk->jghkp', w, eye).reshape(SSM_BLOCKS, BLK_CH, BLK_ST)


def _block_diag_out(w_ghp):
    gpb = N_SSM_GROUPS // SSM_BLOCKS
    w = w_ghp.reshape(SSM_BLOCKS, gpb, SSM_GROUP, SSM_STATE)
    eye = jnp.eye(gpb, dtype=w.dtype)
    return jnp.einsum('jghp,gk->jgpkh', w, eye).reshape(SSM_BLOCKS, BLK_ST, BLK_CH)


def _gelu_tanh(x):
    return 0.5 * x * (1.0 + jnp.tanh(math.sqrt(2.0 / math.pi) * (x + 0.044715 * (x * x * x))))


def _s5_kernel(u_ref, h0_ref, ab_ref, bbig_ref, cbig_ref, d_ref, wglu_ref, bglu_ref,
               y_ref, hl_ref, hst, abr, abi, bu_sc, *, nb, tc, slab):
    c = pl.program_id(0)

    @pl.when(c == 0)
    def _():
        hst[...] = h0_ref[...]
        abr[...] = jnp.broadcast_to(ab_ref[0:1, :], (nb, N_STATE))
        abi[...] = jnp.broadcast_to(ab_ref[1:2, :], (nb, N_STATE))

    u = u_ref[...]
    ub = u.astype(BF16)
    ys = []
    for j in range(SSM_BLOCKS):
        bu_sc[...] = _dot(ub[:, BLK_CH * j:BLK_CH * (j + 1)], bbig_ref[j])
        for s in range(BLK_ST // slab):
            cr = s * slab
            ci = BLK_ST + s * slab
            ar = abr[:, BLK_ST * j + cr:BLK_ST * j + cr + slab]
            ai = abi[:, BLK_ST * j + cr:BLK_ST * j + cr + slab]
            sr = 2 * BLK_ST * j + cr
            si = 2 * BLK_ST * j + ci

            def step(t, carry, ar=ar, ai=ai, cr=cr, ci=ci):
                hr, hi = carry
                r0 = pl.multiple_of(t * nb, nb)
                br = bu_sc[pl.ds(r0, nb), cr:cr + slab]
                bi = bu_sc[pl.ds(r0, nb), ci:ci + slab]
                nr = ar * hr - ai * hi + br
                ni = ar * hi + ai * hr + bi
                bu_sc[pl.ds(r0, nb), cr:cr + slab] = nr
                bu_sc[pl.ds(r0, nb), ci:ci + slab] = ni
                return nr, ni

            hr, hi = lax.fori_loop(0, tc, step, (hst[:, sr:sr + slab], hst[:, si:si + slab]), unroll=4)
            hst[:, sr:sr + slab] = hr
            hst[:, si:si + slab] = hi
        ys.append(_dot(bu_sc[...].astype(BF16), cbig_ref[j]))
    y = jnp.concatenate(ys, axis=1) + d_ref[...] * u
    y = _gelu_tanh(y)
    gate = _dot(y.astype(BF16), wglu_ref[...]) + bglu_ref[...]
    y_ref[...] = (y * jax.nn.sigmoid(gate)).astype(BF16)

    @pl.when(c == pl.num_programs(0) - 1)
    def _():
        hl_ref[...] = hst[...]


def _s5(u_tm, h0, ab, bbig, cbig, d, wglu, bglu, *, nb, seq, tc, slab):
    rows = tc * nb
    return pl.pallas_call(
        functools.partial(_s5_kernel, nb=nb, tc=tc, slab=slab),
        grid=(seq // tc,),
        in_specs=[pl.BlockSpec((rows, D_SSM), lambda c: (c, 0)),
                  _const_spec((nb, 2 * N_STATE)), _const_spec((2, N_STATE)),
                  _const_spec(bbig.shape), _const_spec(cbig.shape),
                  _const_spec((1, D_SSM)), _const_spec((D_SSM, D_SSM)), _const_spec((1, D_SSM))],
        out_specs=(pl.BlockSpec((rows, D_SSM), lambda c: (c, 0)),
                   pl.BlockSpec((nb, 2 * N_STATE), lambda c: (0, 0))),
        out_shape=(jax.ShapeDtypeStruct((seq * nb, D_SSM), BF16),
                   jax.ShapeDtypeStruct((nb, 2 * N_STATE), F32)),
        scratch_shapes=[pltpu.VMEM((nb, 2 * N_STATE), F32), pltpu.VMEM((nb, N_STATE), F32),
                        pltpu.VMEM((nb, N_STATE), F32), pltpu.VMEM((rows, 2 * BLK_ST), F32)],
        compiler_params=pltpu.CompilerParams(dimension_semantics=("arbitrary",),
                                             vmem_limit_bytes=V7X_VMEM_LIMIT),
        name="s5_scan",
    )(u_tm, h0, ab, bbig, cbig, d, wglu, bglu)


def _state_to_cols(s_re, s_im):
    nb = s_re.shape[0]
    r = s_re.reshape(nb, SSM_BLOCKS, 1, BLK_ST)
    i = s_im.reshape(nb, SSM_BLOCKS, 1, BLK_ST)
    return jnp.concatenate([r, i], axis=2).reshape(nb, 2 * N_STATE)


def _cols_to_state(h):
    nb = h.shape[0]
    h4 = h.reshape(nb, SSM_BLOCKS, 2, BLK_ST)
    return (h4[:, :, 0].reshape(nb, N_SSM_GROUPS, SSM_STATE), h4[:, :, 1].reshape(nb, N_SSM_GROUPS, SSM_STATE))


def _two_map_queries(q):
    lane = lax.broadcasted_iota(jnp.int32, q.shape, 1)
    lo = lane < HEAD_DIM
    zero = jnp.zeros_like(q)
    return jnp.concatenate([jnp.where(lo, q, zero), jnp.where(lo, zero, q)], axis=0)


def _diff_lambda(lq_ref, lk_ref, lam_init):
    p = lq_ref[...] * lk_ref[...]
    s = jnp.sum(p, axis=1, keepdims=True)
    e = jnp.exp(s)
    return e[0:1, :] - e[1:2, :] + lam_init


def _softmax_block(s_tiles, vb, shift, m_sc, l_sc, acc_sc):
    m_in = m_sc[...] - shift
    mx = jnp.concatenate([jnp.max(t, axis=0, keepdims=True) for t in s_tiles], axis=1)
    m_new = jnp.maximum(m_in, mx)
    alpha = jnp.exp2(m_in - m_new)
    p_tiles = [jnp.exp2(t - m_new[:, LANES * n:LANES * (n + 1)]) for n, t in enumerate(s_tiles)]
    l_sc[...] = alpha * l_sc[...] + jnp.concatenate([jnp.sum(p, axis=0, keepdims=True) for p in p_tiles], axis=1)
    p = jnp.concatenate(p_tiles, axis=1).astype(BF16)
    acc_sc[...] = alpha * acc_sc[...] + lax.dot_general(vb, p, _TN, preferred_element_type=F32)
    m_sc[...] = m_new


def _diff_finish(acc_t, l, n, lam, g, lam_init):
    o = (acc_t * (1.0 / l)).T
    return _rmsnorm(o[:n] - lam * o[n:], g) * (1.0 - lam_init)


def _in_block_bias(n_keys, n_queries, slope2):
    kl = lax.broadcasted_iota(jnp.int32, (n_keys, n_queries), 0)
    ql = lax.broadcasted_iota(jnp.int32, (n_keys, n_queries), 1)
    return kl, ql, slope2 * (ql - jnp.abs(ql - kl)).astype(F32)


def _attn_prompt_kernel(slopes_ref, q_ref, k_ref, v_ref, lq_ref, lk_ref, g_ref, o_ref,
                        cb_sc, db_sc, m_sc, l_sc, acc_sc, *, bq, lam_init):
    h = pl.program_id(1)
    i = pl.program_id(2)
    slope2 = slopes_ref[h] * LOG2E

    @pl.when(i == 0)
    def _():
        cb_sc[...] = slope2 * lax.broadcasted_iota(jnp.int32, (bq, LANES), 0).astype(F32)
        kl, ql, bias = _in_block_bias(bq, bq, slope2)
        db_sc[...] = jnp.where((kl // CHUNK) <= (ql // CHUNK), bias, -jnp.inf)

    qq = _two_map_queries(q_ref[0])
    m_sc[...] = jnp.full(m_sc.shape, -jnp.inf, F32)
    l_sc[...] = jnp.zeros(l_sc.shape, F32)
    acc_sc[...] = jnp.zeros(acc_sc.shape, F32)
    n_tiles = 2 * bq // LANES
    shift = slope2 * bq

    def step(j, diag):
        start = pl.multiple_of(j * bq, bq)
        kb = k_ref[0, pl.ds(start, bq), :]
        vb = v_ref[0, pl.ds(start, bq), :]
        s = lax.dot_general(kb, qq, _NT, preferred_element_type=F32)
        if diag:
            q_tiles = n_tiles // 2
            tiles = [s[:, LANES * n:LANES * (n + 1)]
                     + db_sc[:, LANES * (n % q_tiles):LANES * (n % q_tiles + 1)] for n in range(n_tiles)]
        else:
            cb = cb_sc[...]
            tiles = [s[:, LANES * n:LANES * (n + 1)] + cb for n in range(n_tiles)]
        _softmax_block(tiles, vb, shift, m_sc, l_sc, acc_sc)

    def body(j, carry):
        step(j, False)
        return carry

    lax.fori_loop(0, i, body, 0)
    step(i, True)
    lam = _diff_lambda(lq_ref, lk_ref, lam_init)
    o_ref[0] = _diff_finish(acc_sc[...], l_sc[...], bq, lam, g_ref[...], lam_init).astype(o_ref.dtype)


def _attn_prompt(qb, kb, vb, slopes, lq, lk, g, *, lam_init, bq=256):
    B, L, _ = qb.shape
    return pl.pallas_call(
        functools.partial(_attn_prompt_kernel, bq=bq, lam_init=lam_init),
        grid=(B, N_HEADS, L // bq),
        in_specs=[pl.BlockSpec(memory_space=pltpu.SMEM),
                  pl.BlockSpec((1, bq, V_DIM), lambda b, h, i: (b, i, h)),
                  pl.BlockSpec((1, L, V_DIM), lambda b, h, i: (b, 0, h)),
                  pl.BlockSpec((1, L, V_DIM), lambda b, h, i: (b, 0, h)),
                  pl.BlockSpec((2, HEAD_DIM), lambda b, h, i: (0, 0)),
                  pl.BlockSpec((2, HEAD_DIM), lambda b, h, i: (0, 0)),
                  pl.BlockSpec((1, V_DIM), lambda b, h, i: (0, 0))],
        out_specs=pl.BlockSpec((1, bq, V_DIM), lambda b, h, i: (b, i, h)),
        out_shape=jax.ShapeDtypeStruct((B, L, D_ATT), BF16),
        scratch_shapes=[pltpu.VMEM((bq, LANES), F32), pltpu.VMEM((bq, bq), F32),
                        pltpu.VMEM((1, 2 * bq), F32), pltpu.VMEM((1, 2 * bq), F32),
                        pltpu.VMEM((V_DIM, 2 * bq), F32)],
        compiler_params=pltpu.CompilerParams(dimension_semantics=("arbitrary", "arbitrary", "arbitrary")),
        name="attn_prompt",
    )(slopes, qb, kb, vb, lq, lk, g)


def _attn_sample_kernel(q_ref, kc_ref, vc_ref, kn_ref, vn_ref, lq_ref, lk_ref, g_ref, o_ref,
                        qq_sc, m_sc, l_sc, acc_sc, *, nq, bk, lam_init):
    j = pl.program_id(1)

    @pl.when(j == 0)
    def _():
        for h in range(N_HEADS):
            qq_sc[h] = _two_map_queries(q_ref[0, :, h * V_DIM:(h + 1) * V_DIM])
        m_sc[...] = jnp.full(m_sc.shape, -jnp.inf, F32)
        l_sc[...] = jnp.zeros(l_sc.shape, F32)
        acc_sc[...] = jnp.zeros(acc_sc.shape, F32)

    def block(k_blk_ref, v_blk_ref, bias_of):
        for h in range(N_HEADS):
            slope2 = 2.0 ** (-8.0 * (h + 1) / N_HEADS) * LOG2E
            kb = k_blk_ref[0, :, h, :].astype(BF16)
            vb = v_blk_ref[0, :, h, :].astype(BF16)
            s = lax.dot_general(kb, qq_sc[h], _NT, preferred_element_type=F32)
            _softmax_block([s + bias_of(slope2)], vb, slope2 * bk, m_sc.at[h], l_sc.at[h], acc_sc.at[h])

    krow = lax.broadcasted_iota(jnp.int32, (bk, 2 * nq), 0).astype(F32)
    block(kc_ref, vc_ref, lambda slope2: slope2 * krow)

    @pl.when(j == pl.num_programs(1) - 1)
    def _():
        kl = lax.broadcasted_iota(jnp.int32, (nq, 2 * nq), 0)
        lane = lax.broadcasted_iota(jnp.int32, (nq, 2 * nq), 1)
        ql = jnp.where(lane >= nq, lane - nq, lane)
        rel = (ql - jnp.abs(ql - kl)).astype(F32)
        block(kn_ref, vn_ref, lambda slope2: slope2 * rel)
        lam = _diff_lambda(lq_ref, lk_ref, lam_init)
        for h in range(N_HEADS):
            o = _diff_finish(acc_sc[h], l_sc[h], nq, lam, g_ref[...], lam_init)
            o_ref[0, :, h * V_DIM:(h + 1) * V_DIM] = o.astype(o_ref.dtype)


def _attn_sample(qb, k_cache, v_cache, k_new, v_new, lq, lk, g, *, lam_init, bk=1024):
    B, nq, _ = qb.shape
    past = k_cache.shape[1]
    q_spec = pl.BlockSpec((1, nq, D_ATT), lambda b, j: (b, 0, 0))
    new_spec = pl.BlockSpec((1, nq, N_HEADS, V_DIM), lambda b, j: (b, 0, 0, 0))
    cache_spec = pl.BlockSpec((1, bk, N_HEADS, V_DIM), lambda b, j: (b, j, 0, 0))
    small = lambda shape: pl.BlockSpec(shape, lambda b, j: (0, 0))
    return pl.pallas_call(
        functools.partial(_attn_sample_kernel, nq=nq, bk=bk, lam_init=lam_init),
        grid=(B, past // bk),
        in_specs=[q_spec, cache_spec, cache_spec, new_spec, new_spec,
                  small((2, HEAD_DIM)), small((2, HEAD_DIM)), small((1, V_DIM))],
        out_specs=q_spec,
        out_shape=jax.ShapeDtypeStruct((B, nq, D_ATT), BF16),
        scratch_shapes=[pltpu.VMEM((N_HEADS, 2 * nq, V_DIM), BF16), pltpu.VMEM((N_HEADS, 1, 2 * nq), F32),
                        pltpu.VMEM((N_HEADS, 1, 2 * nq), F32), pltpu.VMEM((N_HEADS, V_DIM, 2 * nq), F32)],
        compiler_params=pltpu.CompilerParams(dimension_semantics=("arbitrary", "arbitrary"),
                                             vmem_limit_bytes=V7X_VMEM_LIMIT),
        name="attn_sample",
    )(qb, k_cache, v_cache, k_new, v_new, lq, lk, g)


def _mem_kv_kernel(m_ref, g_ref, wck_ref, wcv_ref, mk_ref, mv_ref, mkb_ref, mvb_ref):
    mn = _rmsnorm(m_ref[...], g_ref[...]).astype(BF16)
    k = _dot(mn, wck_ref[...])
    v = _dot(mn, wcv_ref[...])
    _store_heads(mk_ref, k, N_MEM_HEADS, MEM_HEAD_DIM)
    _store_heads(mv_ref, v, N_MEM_HEADS, MEM_HEAD_DIM)
    mkb_ref[...] = k.astype(BF16)
    mvb_ref[...] = v.astype(BF16)


def _mem_kv(mem2d, g, wck, wcv, *, tm=512):
    T = mem2d.shape[0]
    tok = pl.BlockSpec((tm, D_MODEL), lambda i: (i, 0))
    heads = pl.BlockSpec((tm, N_MEM_HEADS, MEM_HEAD_DIM), lambda i: (i, 0, 0))
    f = jax.ShapeDtypeStruct((T, N_MEM_HEADS, MEM_HEAD_DIM), F32)
    b = jax.ShapeDtypeStruct((T, D_MODEL), BF16)
    return pl.pallas_call(
        _mem_kv_kernel,
        grid=(T // tm,),
        in_specs=[tok, _const_spec((1, D_MODEL)), _const_spec(wck.shape), _const_spec(wcv.shape)],
        out_specs=(heads, heads, tok, tok),
        out_shape=(f, f, b, b),
        compiler_params=pltpu.CompilerParams(dimension_semantics=("arbitrary",),
                                             vmem_limit_bytes=V7X_VMEM_LIMIT),
        name="mem_kv",
    )(mem2d, g, wck, wcv)


def _tail_kernel(h_ref, ys_ref, ya_ref, mk_ref, mv_ref, wout_ref, gc_ref, wcq_ref, wco_ref,
                 g2_ref, wgu_ref, wd_ref, gf_ref, y_ref, *, nb, rpb, ff_chunks, final_norm):
    if nb == 1:
        ys = ys_ref[...]
    else:
        ys = jnp.concatenate([ys_ref[:, lb * D_SSM:(lb + 1) * D_SSM] for lb in range(nb)], axis=0)
    h = h_ref[...] + _dot(ys, wout_ref[:D_SSM, :]) + _dot(ya_ref[...], wout_ref[D_SSM:, :])

    qn = _rmsnorm(h, gc_ref[...]).astype(BF16)
    q = (_dot(qn, wcq_ref[...]) * (MEM_HEAD_DIM ** -0.5 * LOG2E)).astype(BF16)
    rows = []
    for lb in range(nb):
        heads = []
        for hh in range(N_MEM_HEADS):
            cols = slice(hh * MEM_HEAD_DIM, (hh + 1) * MEM_HEAD_DIM)
            qh = q[lb * rpb:(lb + 1) * rpb, cols]
            s = lax.dot_general(qh, mk_ref[lb, :, cols], _NT, preferred_element_type=F32)
            p = jnp.exp2(s - jnp.max(s, axis=1, keepdims=True))
            o = _dot(p.astype(BF16), mv_ref[lb, :, cols]) / jnp.sum(p, axis=1, keepdims=True)
            heads.append(o)
        rows.append(jnp.concatenate(heads, axis=1))
    o = rows[0] if nb == 1 else jnp.concatenate(rows, axis=0)
    h = h + _dot(o.astype(BF16), wco_ref[...])

    xn = _rmsnorm(h, g2_ref[...]).astype(BF16)
    x2 = h + 0.5 * _swiglu_half(xn, wgu_ref, wd_ref, ff_chunks)
    y_ref[...] = _rmsnorm(x2, gf_ref[...]) if final_norm else x2


def _tail(h2d, ys_tm, ya2d, mkb, mvb, wout, gc, wcq, wco, g2, wgu, wd, gf, *, n_streams, seq, tm, final_norm,
          ff_chunks=2):
    T = h2d.shape[0]
    if seq >= tm:
        nb, rpb = 1, tm
        nt = seq // tm
        ys_map = lambda i: (i % nt, i // nt)
        mem_map = lambda i: (i // nt, 0, 0)
    else:
        nb, rpb = tm // seq, seq
        ys_map = lambda i: (0, i)
        mem_map = lambda i: (i, 0, 0)
    tok = lambda w: pl.BlockSpec((tm, w), lambda i: (i, 0))
    mem_spec = pl.BlockSpec((nb, N_MEM, D_MODEL), mem_map)
    return pl.pallas_call(
        functools.partial(_tail_kernel, nb=nb, rpb=rpb, ff_chunks=ff_chunks, final_norm=final_norm),
        grid=(T // tm,),
        in_specs=[tok(D_MODEL), pl.BlockSpec((rpb, nb * D_SSM), ys_map), tok(D_ATT), mem_spec, mem_spec,
                  _const_spec(wout.shape), _const_spec((1, D_MODEL)), _const_spec(wcq.shape),
                  _const_spec(wco.shape), _const_spec((1, D_MODEL)), _const_spec(wgu.shape),
                  _const_spec(wd.shape), _const_spec((1, D_MODEL))],
        out_specs=tok(D_MODEL),
        out_shape=jax.ShapeDtypeStruct((T, D_MODEL), F32),
        compiler_params=pltpu.CompilerParams(dimension_semantics=("arbitrary",),
                                             vmem_limit_bytes=V7X_VMEM_LIMIT),
        name="tail",
    )(h2d, ys_tm, ya2d, mkb, mvb, wout, gc, wcq, wco, g2, wgu, wd, gf)


def kernel(x_prompt, x_sample, cache_attn_k, cache_attn_v, state_s5_re, state_s5_im, cache_mem_k, cache_mem_v, mem_prompt, g_ffn1, w_ffn1_gu, w_ffn1_d, g_mix, w_in, ssm_a_re, ssm_a_im, ssm_log_dt, ssm_b_re, ssm_b_im, ssm_c_re, ssm_c_im, ssm_d, w_glu, b_glu, lambda_q, lambda_k, g_subln, w_out, g_mem, g_cross, w_cq, w_ck, w_cv, w_co, g_ffn2, w_ffn2_gu, w_ffn2_d, g_final):
    depth = g_ffn1.shape[0]
    B, L, D = x_prompt.shape
    SB, SL, _ = x_sample.shape
    row = lambda a: a.reshape(1, -1)
    slopes = jnp.asarray([2.0 ** (-8.0 * (i + 1) / N_HEADS) for i in range(N_HEADS)], F32)

    xp = x_prompt.reshape(B * L, D)
    xs = x_sample.reshape(SB * SL, D)
    outs = [[] for _ in range(10)]
    for l in range(depth):
        lam_init = 0.8 - 0.6 * math.exp(-0.3 * l)
        wgu1, wd1, win = w_ffn1_gu[l].astype(BF16), w_ffn1_d[l].astype(BF16), w_in[l].astype(BF16)
        wgu2, wd2 = w_ffn2_gu[l].astype(BF16), w_ffn2_d[l].astype(BF16)
        wout, wcq, wco = w_out[l].astype(BF16), w_cq[l].astype(BF16), w_co[l].astype(BF16)
        wck, wcv, wglu = w_ck[l].astype(BF16), w_cv[l].astype(BF16), w_glu[l].astype(BF16)

        abr, abi, bbr, bbi = _s5_prep(ssm_a_re[l], ssm_a_im[l], ssm_log_dt[l], ssm_b_re[l], ssm_b_im[l])
        ab = jnp.stack([abr.reshape(N_STATE), abi.reshape(N_STATE)])
        bbig = jnp.concatenate([_block_diag_in(bbr), _block_diag_in(bbi)], axis=2).astype(BF16)
        cbig = jnp.concatenate([_block_diag_out(ssm_c_re[l]), -_block_diag_out(ssm_c_im[l])], axis=1).astype(BF16)
        s5_w = (ab, bbig, cbig, row(ssm_d[l]), wglu, row(b_glu[l]))
        att_w = (lambda_q[l], lambda_k[l], row(g_subln[l]))
        tail_w = (wout, row(g_cross[l]), wcq, wco, row(g_ffn2[l]), wgu2, wd2, row(g_final))
        last = l == depth - 1

        hp, up, qp, kp, vp, kpb, vpb = _ffn_mix(xp, row(g_ffn1[l]), wgu1, wd1, row(g_mix[l]), win,
                                                 n_streams=B, seq=L, tm=512, bf16_kv=True)
        zeros = jnp.zeros((B, 2 * N_STATE), F32)
        yp_ssm, hlp = _s5(up.reshape(L * B, D_SSM), zeros, *s5_w, nb=B, seq=L, tc=64, slab=512)
        yp_att = _attn_prompt(qp.reshape(B, L, D_ATT), kpb.reshape(B, L, D_ATT), vpb.reshape(B, L, D_ATT),
                              slopes, *att_w, lam_init=lam_init)
        mkp, mvp, mkpb, mvpb = _mem_kv(mem_prompt.reshape(B * N_MEM, D), row(g_mem[l]), wck, wcv)
        xp = _tail(hp, yp_ssm.reshape(L, B * D_SSM), yp_att.reshape(B * L, D_ATT),
                   mkpb.reshape(B, N_MEM, D), mvpb.reshape(B, N_MEM, D), *tail_w,
                   n_streams=B, seq=L, tm=512, final_norm=last)

        hs, us, qs, ks, vs = _ffn_mix(xs, row(g_ffn1[l]), wgu1, wd1, row(g_mix[l]), win,
                                      n_streams=SB, seq=SL, tm=512, bf16_kv=False)
        ks = ks.reshape(SB, SL, N_HEADS, V_DIM)
        vs = vs.reshape(SB, SL, N_HEADS, V_DIM)
        h0 = _state_to_cols(state_s5_re[l], state_s5_im[l])
        ys_ssm, hls = _s5(us.reshape(SL * SB, D_SSM), h0, *s5_w, nb=SB, seq=SL, tc=SL, slab=256)
        ys_att = _attn_sample(qs.reshape(SB, SL, D_ATT), cache_attn_k[l], cache_attn_v[l], ks, vs,
                              *att_w, lam_init=lam_init)
        xs = _tail(hs, ys_ssm.reshape(SL, SB * D_SSM), ys_att.reshape(SB * SL, D_ATT),
                   cache_mem_k[l].reshape(SB, N_MEM, D).astype(BF16),
                   cache_mem_v[l].reshape(SB, N_MEM, D).astype(BF16), *tail_w,
                   n_streams=SB, seq=SL, tm=256, final_norm=last)

        rep, imp = _cols_to_state(hlp)
        res, ims = _cols_to_state(hls)
        layer = (kp.reshape(B, L, N_HEADS, V_DIM), vp.reshape(B, L, N_HEADS, V_DIM), rep, imp,
                 mkp.reshape(B, N_MEM, N_MEM_HEADS, MEM_HEAD_DIM), mvp.reshape(B, N_MEM, N_MEM_HEADS, MEM_HEAD_DIM),
                 ks, vs, res, ims)
        for o, a in zip(outs, layer):
            o.append(a)

    return (xp.reshape(B, L, D), xs.reshape(SB, SL, D)) + tuple(jnp.stack(o) for o in outs)
```

```python
import functools
import math

import jax
import jax.numpy as jnp
from jax import lax
from jax.experimental import pallas as pl
from jax.experimental.pallas import tpu as pltpu

F32 = jnp.float32
BF16 = jnp.bfloat16

D_MODEL = 1024
CHUNK = 64
D_SSM = 512
SSM_GROUP = 16
N_SSM_GROUPS = 32
SSM_STATE = 64
D_ATT = 512
N_HEADS = 4
HEAD_DIM = 64
V_DIM = 128
D_IN = D_SSM + 3 * D_ATT
N_MEM = 256
N_MEM_HEADS = 4
MEM_HEAD_DIM = 256
D_FF = 2816
EPS = 1e-6
LOG2E = math.log2(math.e)

N_STATE = N_SSM_GROUPS * SSM_STATE
SSM_BLOCKS = 2
BLK_CH = D_SSM // SSM_BLOCKS
BLK_ST = N_STATE // SSM_BLOCKS

LANES = 128
V7X_VMEM_LIMIT = 56 * 1024 * 1024

_NT = (((1,), (1,)), ((), ()))
_TN = (((0,), (0,)), ((), ()))


def _rmsnorm(x, g):
    return x * lax.rsqrt(jnp.mean(x * x, axis=-1, keepdims=True) + EPS) * g


def _dot(a, b):
    return jnp.dot(a, b, preferred_element_type=F32)


def _const_spec(shape):
    nd = len(shape)
    return pl.BlockSpec(shape, lambda *_: (0,) * nd, pipeline_mode=pl.Buffered(1))


def _swiglu_half(xn, wgu_ref, wd_ref, ff_chunks):
    cw = D_FF // ff_chunks
    acc = None
    for c in range(ff_chunks):
        gate = _dot(xn, wgu_ref[:, c * cw:(c + 1) * cw])
        up = _dot(xn, wgu_ref[:, D_FF + c * cw:D_FF + (c + 1) * cw])
        act = (gate * jax.nn.sigmoid(gate) * up).astype(BF16)
        part = _dot(act, wd_ref[c * cw:(c + 1) * cw, :])
        acc = part if acc is None else acc + part
    return acc


def _store_heads(ref, x, n_heads, width):
    for hh in range(n_heads):
        ref[:, hh, :] = x[:, hh * width:(hh + 1) * width]


def _ffn_mix_kernel(x_ref, g1_ref, wgu_ref, wd_ref, gm_ref, win_ref,
                    h_ref, u_ref, qb_ref, k_ref, v_ref, *kvb_refs, nb, rpb, ff_chunks):
    x = x_ref[...]
    xn = _rmsnorm(x, g1_ref[...]).astype(BF16)
    h = x + 0.5 * _swiglu_half(xn, wgu_ref, wd_ref, ff_chunks)
    h_ref[...] = h
    hn = _rmsnorm(h, gm_ref[...]).astype(BF16)
    z = _dot(hn, win_ref[...])
    for lb in range(nb):
        u_ref[:, lb * D_SSM:(lb + 1) * D_SSM] = z[lb * rpb:(lb + 1) * rpb, :D_SSM]
    q = z[:, D_SSM:D_SSM + D_ATT]
    k = z[:, D_SSM + D_ATT:D_SSM + 2 * D_ATT]
    v = z[:, D_SSM + 2 * D_ATT:]
    qb_ref[...] = (q * (HEAD_DIM ** -0.5 * LOG2E)).astype(BF16)
    _store_heads(k_ref, k, N_HEADS, V_DIM)
    _store_heads(v_ref, v, N_HEADS, V_DIM)
    if kvb_refs:
        kb_ref, vb_ref = kvb_refs
        kb_ref[...] = k.astype(BF16)
        vb_ref[...] = v.astype(BF16)


def _ffn_mix(x2d, g1, wgu, wd, gm, win, *, n_streams, seq, tm, bf16_kv, ff_chunks=2):
    T = x2d.shape[0]
    if seq >= tm:
        nb, rpb = 1, tm
        nt = seq // tm
        u_map = lambda i: (i % nt, i // nt)
    else:
        nb, rpb = tm // seq, seq
        u_map = lambda i: (0, i)
    tok = lambda w: pl.BlockSpec((tm, w), lambda i: (i, 0))
    heads = pl.BlockSpec((tm, N_HEADS, V_DIM), lambda i: (i, 0, 0))
    out_shape = [
        jax.ShapeDtypeStruct((T, D_MODEL), F32),
        jax.ShapeDtypeStruct((seq, n_streams * D_SSM), F32),
        jax.ShapeDtypeStruct((T, D_ATT), BF16),
        jax.ShapeDtypeStruct((T, N_HEADS, V_DIM), F32),
        jax.ShapeDtypeStruct((T, N_HEADS, V_DIM), F32),
    ]
    out_specs = [tok(D_MODEL), pl.BlockSpec((rpb, nb * D_SSM), u_map), tok(D_ATT), heads, heads]
    if bf16_kv:
        out_shape += [jax.ShapeDtypeStruct((T, D_ATT), BF16)] * 2
        out_specs += [tok(D_ATT)] * 2
    return pl.pallas_call(
        functools.partial(_ffn_mix_kernel, nb=nb, rpb=rpb, ff_chunks=ff_chunks),
        grid=(T // tm,),
        in_specs=[tok(D_MODEL), _const_spec((1, D_MODEL)), _const_spec(wgu.shape), _const_spec(wd.shape),
                  _const_spec((1, D_MODEL)), _const_spec(win.shape)],
        out_specs=tuple(out_specs),
        out_shape=tuple(out_shape),
        compiler_params=pltpu.CompilerParams(dimension_semantics=("arbitrary",),
                                             vmem_limit_bytes=V7X_VMEM_LIMIT),
        name="ffn_mix",
    )(x2d, g1, wgu, wd, gm, win)


def _s5_prep_kernel(are_ref, aim_ref, ldt_ref, bre_ref, bim_ref, abr_ref, abi_ref, bbr_ref, bbi_ref):
    lre = are_ref[...]
    lim = aim_ref[...]
    dt = jnp.exp(ldt_ref[...])
    mag = jnp.exp(lre * dt)
    ar = mag * jnp.cos(lim * dt)
    ai = mag * jnp.sin(lim * dt)
    abr_ref[...] = ar
    abi_ref[...] = ai
    den = lre * lre + lim * lim
    xr = ar - 1.0
    cr = (xr * lre + ai * lim) / den
    ci = (ai * lre - xr * lim) / den
    br = bre_ref[...]
    bi = bim_ref[...]
    bbr_ref[...] = cr * br - ci * bi
    bbi_ref[...] = cr * bi + ci * br


def _s5_prep(a_re, a_im, log_dt, b_re, b_im):
    G, P, H = N_SSM_GROUPS, SSM_STATE, SSM_GROUP
    per_row = lambda a: jnp.repeat(a, H, axis=0)
    b_re_t = jnp.swapaxes(b_re, 1, 2).reshape(G * H, P)
    b_im_t = jnp.swapaxes(b_im, 1, 2).reshape(G * H, P)
    o = jax.ShapeDtypeStruct((G * H, P), F32)
    abr, abi, bbr, bbi = pl.pallas_call(
        _s5_prep_kernel,
        out_shape=(o, o, o, o),
        name="s5_prep",
    )(per_row(a_re), per_row(a_im), per_row(log_dt.reshape(G, 1)), b_re_t, b_im_t)
    return (abr.reshape(G, H, P)[:, 0], abi.reshape(G, H, P)[:, 0], bbr.reshape(G, H, P), bbi.reshape(G, H, P))


def _block_diag_in(w_ghp):
    gpb = N_SSM_GROUPS // SSM_BLOCKS
    w = w_ghp.reshape(SSM_BLOCKS, gpb, SSM_GROUP, SSM_STATE)
    eye = jnp.eye(gpb, dtype=w.dtype)
    return jnp.einsum('jghp,gk->jghkp', w, eye).reshape(SSM_BLOCKS, BLK_CH, BLK_ST)


def _block_diag_out(w_ghp):
    gpb = N_SSM_GROUPS // SSM_BLOCKS
    w = w_ghp.reshape(SSM_BLOCKS, gpb, SSM_GROUP, SSM_STATE)
    eye = jnp.eye(gpb, dtype=w.dtype)
    return jnp.einsum('jghp,gk->jgpkh', w, eye).reshape(SSM_BLOCKS, BLK_ST, BLK_CH)


def _gelu_tanh(x):
    return 0.5 * x * (1.0 + jnp.tanh(math.sqrt(2.0 / math.pi) * (x + 0.044715 * (x * x * x))))


def _s5_kernel(u_ref, h0_ref, ab_ref, bbig_ref, cbig_ref, d_ref, wglu_ref, bglu_ref,
               y_ref, hl_ref, hst, abr, abi, bu_sc, *, nb, tc, slab):
    c = pl.program_id(0)

    @pl.when(c == 0)
    def _():
        hst[...] = h0_ref[...]
        abr[...] = jnp.broadcast_to(ab_ref[0:1, :], (nb, N_STATE))
        abi[...] = jnp.broadcast_to(ab_ref[1:2, :], (nb, N_STATE))

    u = u_ref[...]
    ub = u.astype(BF16)
    ys = []
    for j in range(SSM_BLOCKS):
        bu_sc[...] = _dot(ub[:, BLK_CH * j:BLK_CH * (j + 1)], bbig_ref[j])
        for s in range(BLK_ST // slab):
            cr = s * slab
            ci = BLK_ST + s * slab
            ar = abr[:, BLK_ST * j + cr:BLK_ST * j + cr + slab]
            ai = abi[:, BLK_ST * j + cr:BLK_ST * j + cr + slab]
            sr = 2 * BLK_ST * j + cr
            si = 2 * BLK_ST * j + ci

            def step(t, carry, ar=ar, ai=ai, cr=cr, ci=ci):
                hr, hi = carry
                r0 = pl.multiple_of(t * nb, nb)
                br = bu_sc[pl.ds(r0, nb), cr:cr + slab]
                bi = bu_sc[pl.ds(r0, nb), ci:ci + slab]
                nr = ar * hr - ai * hi + br
                ni = ar * hi + ai * hr + bi
                bu_sc[pl.ds(r0, nb), cr:cr + slab] = nr
                bu_sc[pl.ds(r0, nb), ci:ci + slab] = ni
                return nr, ni

            hr, hi = lax.fori_loop(0, tc, step, (hst[:, sr:sr + slab], hst[:, si:si + slab]), unroll=4)
            hst[:, sr:sr + slab] = hr
            hst[:, si:si + slab] = hi
        ys.append(_dot(bu_sc[...].astype(BF16), cbig_ref[j]))
    y = jnp.concatenate(ys, axis=1) + d_ref[...] * u
    y = _gelu_tanh(y)
    gate = _dot(y.astype(BF16), wglu_ref[...]) + bglu_ref[...]
    y_ref[...] = (y * jax.nn.sigmoid(gate)).astype(BF16)

    @pl.when(c == pl.num_programs(0) - 1)
    def _():
        hl_ref[...] = hst[...]


def _s5(u_tm, h0, ab, bbig, cbig, d, wglu, bglu, *, nb, seq, tc, slab):
    rows = tc * nb
    return pl.pallas_call(
        functools.partial(_s5_kernel, nb=nb, tc=tc, slab=slab),
        grid=(seq // tc,),
        in_specs=[pl.BlockSpec((rows, D_SSM), lambda c: (c, 0)),
                  _const_spec((nb, 2 * N_STATE)), _const_spec((2, N_STATE)),
                  _const_spec(bbig.shape), _const_spec(cbig.shape),
                  _const_spec((1, D_SSM)), _const_spec((D_SSM, D_SSM)), _const_spec((1, D_SSM))],
        out_specs=(pl.BlockSpec((rows, D_SSM), lambda c: (c, 0)),
                   pl.BlockSpec((nb, 2 * N_STATE), lambda c: (0, 0))),
        out_shape=(jax.ShapeDtypeStruct((seq * nb, D_SSM), BF16),
                   jax.ShapeDtypeStruct((nb, 2 * N_STATE), F32)),
        scratch_shapes=[pltpu.VMEM((nb, 2 * N_STATE), F32), pltpu.VMEM((nb, N_STATE), F32),
                        pltpu.VMEM((nb, N_STATE), F32), pltpu.VMEM((rows, 2 * BLK_ST), F32)],
        compiler_params=pltpu.CompilerParams(dimension_semantics=("arbitrary",),
                                             vmem_limit_bytes=V7X_VMEM_LIMIT),
        name="s5_scan",
    )(u_tm, h0, ab, bbig, cbig, d, wglu, bglu)


def _state_to_cols(s_re, s_im):
    nb = s_re.shape[0]
    r = s_re.reshape(nb, SSM_BLOCKS, 1, BLK_ST)
    i = s_im.reshape(nb, SSM_BLOCKS, 1, BLK_ST)
    return jnp.concatenate([r, i], axis=2).reshape(nb, 2 * N_STATE)


def _cols_to_state(h):
    nb = h.shape[0]
    h4 = h.reshape(nb, SSM_BLOCKS, 2, BLK_ST)
    return (h4[:, :, 0].reshape(nb, N_SSM_GROUPS, SSM_STATE), h4[:, :, 1].reshape(nb, N_SSM_GROUPS, SSM_STATE))


def _two_map_queries(q):
    lane = lax.broadcasted_iota(jnp.int32, q.shape, 1)
    lo = lane < HEAD_DIM
    zero = jnp.zeros_like(q)
    return jnp.concatenate([jnp.where(lo, q, zero), jnp.where(lo, zero, q)], axis=0)


def _diff_lambda(lq_ref, lk_ref, lam_init):
    p = lq_ref[...] * lk_ref[...]
    s = jnp.sum(p, axis=1, keepdims=True)
    e = jnp.exp(s)
    return e[0:1, :] - e[1:2, :] + lam_init


def _softmax_update(s_tiles, shift, m_sc, l_sc):
    m_in = m_sc[...] - shift
    mx = jnp.concatenate([jnp.max(t, axis=0, keepdims=True) for t in s_tiles], axis=1)
    m_new = jnp.maximum(m_in, mx)
    alpha = jnp.exp2(m_in - m_new)
    p_tiles = [jnp.exp2(t - m_new[:, LANES * n:LANES * (n + 1)]) for n, t in enumerate(s_tiles)]
    l_sc[...] = alpha * l_sc[...] + jnp.concatenate([jnp.sum(p, axis=0, keepdims=True) for p in p_tiles], axis=1)
    m_sc[...] = m_new
    return jnp.concatenate(p_tiles, axis=1).astype(BF16), alpha


def _accumulate_values(vb, p, alpha, acc_sc):
    acc_sc[...] = alpha * acc_sc[...] + lax.dot_general(vb, p, _TN, preferred_element_type=F32)


def _softmax_block(s_tiles, vb, shift, m_sc, l_sc, acc_sc):
    p, alpha = _softmax_update(s_tiles, shift, m_sc, l_sc)
    _accumulate_values(vb, p, alpha, acc_sc)


def _diff_finish(acc_t, l, n, lam, g, lam_init):
    o = (acc_t * (1.0 / l)).T
    return _rmsnorm(o[:n] - lam * o[n:], g) * (1.0 - lam_init)


def _in_block_bias(n_keys, n_queries, slope2):
    kl = lax.broadcasted_iota(jnp.int32, (n_keys, n_queries), 0)
    ql = lax.broadcasted_iota(jnp.int32, (n_keys, n_queries), 1)
    return kl, ql, slope2 * (ql - jnp.abs(ql - kl)).astype(F32)


def _attn_prompt_kernel(q_ref, k_ref, v_ref, lq_ref, lk_ref, g_ref, o_ref,
                        qq_sc, cb_sc, db_sc, m_sc, l_sc, acc_sc, *, bq, lam_init):
    i = pl.program_id(1)
    slopes2 = [2.0 ** (-8.0 * (h + 1) / N_HEADS) * LOG2E for h in range(N_HEADS)]

    @pl.when((pl.program_id(0) == 0) & (i == 0))
    def _():
        krow = lax.broadcasted_iota(jnp.int32, (bq, LANES), 0).astype(F32)
        kl, ql, rel = _in_block_bias(bq, bq, 1.0)
        visible = (kl // CHUNK) <= (ql // CHUNK)
        for h in range(N_HEADS):
            cb_sc[h] = slopes2[h] * krow
            db_sc[h] = jnp.where(visible, slopes2[h] * rel, -jnp.inf)

    for h in range(N_HEADS):
        qq_sc[h] = _two_map_queries(q_ref[0, :, h * V_DIM:(h + 1) * V_DIM])
    m_sc[...] = jnp.full(m_sc.shape, -jnp.inf, F32)
    l_sc[...] = jnp.zeros(l_sc.shape, F32)
    acc_sc[...] = jnp.zeros(acc_sc.shape, F32)
    n_tiles = 2 * bq // LANES
    q_tiles = n_tiles // 2

    def step(j, diag):
        rows = pl.ds(pl.multiple_of(j * bq, bq), bq)
        cols = [slice(h * V_DIM, (h + 1) * V_DIM) for h in range(N_HEADS)]
        scores = [lax.dot_general(k_ref[0, rows, cols[h]], qq_sc[h], _NT, preferred_element_type=F32)
                  for h in range(N_HEADS)]
        probs = []
        for h, s in enumerate(scores):
            if diag:
                tiles = [s[:, LANES * n:LANES * (n + 1)]
                         + db_sc[h, :, LANES * (n % q_tiles):LANES * (n % q_tiles + 1)] for n in range(n_tiles)]
            else:
                tiles = [s[:, LANES * n:LANES * (n + 1)] + cb_sc[h] for n in range(n_tiles)]
            probs.append(_softmax_update(tiles, slopes2[h] * bq, m_sc.at[h], l_sc.at[h]))
        for h, (p, alpha) in enumerate(probs):
            _accumulate_values(v_ref[0, rows, cols[h]], p, alpha, acc_sc.at[h])

    def body(j, carry):
        step(j, False)
        return carry

    lax.fori_loop(0, i, body, 0)
    step(i, True)
    lam = _diff_lambda(lq_ref, lk_ref, lam_init)
    for h in range(N_HEADS):
        o = _diff_finish(acc_sc[h], l_sc[h], bq, lam, g_ref[...], lam_init)
        o_ref[0, :, h * V_DIM:(h + 1) * V_DIM] = o.astype(o_ref.dtype)


def _attn_prompt(qb, kb, vb, lq, lk, g, *, lam_init, bq=256):
    B, L, _ = qb.shape
    small = lambda shape: pl.BlockSpec(shape, lambda b, i: (0, 0))
    return pl.pallas_call(
        functools.partial(_attn_prompt_kernel, bq=bq, lam_init=lam_init),
        grid=(B, L // bq),
        in_specs=[pl.BlockSpec((1, bq, D_ATT), lambda b, i: (b, i, 0)),
                  pl.BlockSpec((1, L, D_ATT), lambda b, i: (b, 0, 0)),
                  pl.BlockSpec((1, L, D_ATT), lambda b, i: (b, 0, 0)),
                  small((2, HEAD_DIM)), small((2, HEAD_DIM)), small((1, V_DIM))],
        out_specs=pl.BlockSpec((1, bq, D_ATT), lambda b, i: (b, i, 0)),
        out_shape=jax.ShapeDtypeStruct((B, L, D_ATT), BF16),
        scratch_shapes=[pltpu.VMEM((N_HEADS, 2 * bq, V_DIM), BF16),
                        pltpu.VMEM((N_HEADS, bq, LANES), F32), pltpu.VMEM((N_HEADS, bq, bq), F32),
                        pltpu.VMEM((N_HEADS, 1, 2 * bq), F32), pltpu.VMEM((N_HEADS, 1, 2 * bq), F32),
                        pltpu.VMEM((N_HEADS, V_DIM, 2 * bq), F32)],
        compiler_params=pltpu.CompilerParams(dimension_semantics=("arbitrary", "arbitrary")),
        name="attn_prompt",
    )(qb, kb, vb, lq, lk, g)


def _attn_sample_kernel(q_ref, kc_ref, vc_ref, kn_ref, vn_ref, lq_ref, lk_ref, g_ref, o_ref,
                        qq_sc, m_sc, l_sc, acc_sc, *, nq, bk, lam_init):
    j = pl.program_id(1)

    @pl.when(j == 0)
    def _():
        for h in range(N_HEADS):
            qq_sc[h] = _two_map_queries(q_ref[0, :, h * V_DIM:(h + 1) * V_DIM])
        m_sc[...] = jnp.full(m_sc.shape, -jnp.inf, F32)
        l_sc[...] = jnp.zeros(l_sc.shape, F32)
        acc_sc[...] = jnp.zeros(acc_sc.shape, F32)

    def block(k_blk_ref, v_blk_ref, n_keys, bias_of):
        for h in range(N_HEADS):
            slope2 = 2.0 ** (-8.0 * (h + 1) / N_HEADS) * LOG2E
            kb = k_blk_ref[0, pl.ds(h, n_keys, stride=N_HEADS), :].astype(BF16)
            vb = v_blk_ref[0, pl.ds(h, n_keys, stride=N_HEADS), :].astype(BF16)
            s = lax.dot_general(kb, qq_sc[h], _NT, preferred_element_type=F32)
            _softmax_block([s + bias_of(slope2)], vb, slope2 * bk, m_sc.at[h], l_sc.at[h], acc_sc.at[h])

    krow = lax.broadcasted_iota(jnp.int32, (bk, 2 * nq), 0).astype(F32)
    block(kc_ref, vc_ref, bk, lambda slope2: slope2 * krow)

    @pl.when(j == pl.num_programs(1) - 1)
    def _():
        kl = lax.broadcasted_iota(jnp.int32, (nq, 2 * nq), 0)
        lane = lax.broadcasted_iota(jnp.int32, (nq, 2 * nq), 1)
        ql = jnp.where(lane >= nq, lane - nq, lane)
        rel = (ql - jnp.abs(ql - kl)).astype(F32)
        block(kn_ref, vn_ref, nq, lambda slope2: slope2 * rel)
        lam = _diff_lambda(lq_ref, lk_ref, lam_init)
        for h in range(N_HEADS):
            o = _diff_finish(acc_sc[h], l_sc[h], nq, lam, g_ref[...], lam_init)
            o_ref[0, :, h * V_DIM:(h + 1) * V_DIM] = o.astype(o_ref.dtype)


def _attn_sample(qb, k_cache, v_cache, k_new, v_new, lq, lk, g, *, lam_init, bk=1024):
    B, nq, _ = qb.shape
    past = k_cache.shape[1] // N_HEADS
    q_spec = pl.BlockSpec((1, nq, D_ATT), lambda b, j: (b, 0, 0))
    new_spec = pl.BlockSpec((1, nq * N_HEADS, V_DIM), lambda b, j: (b, 0, 0))
    cache_spec = pl.BlockSpec((1, bk * N_HEADS, V_DIM), lambda b, j: (b, j, 0))
    small = lambda shape: pl.BlockSpec(shape, lambda b, j: (0, 0))
    return pl.pallas_call(
        functools.partial(_attn_sample_kernel, nq=nq, bk=bk, lam_init=lam_init),
        grid=(B, past // bk),
        in_specs=[q_spec, cache_spec, cache_spec, new_spec, new_spec,
                  small((2, HEAD_DIM)), small((2, HEAD_DIM)), small((1, V_DIM))],
        out_specs=q_spec,
        out_shape=jax.ShapeDtypeStruct((B, nq, D_ATT), BF16),
        scratch_shapes=[pltpu.VMEM((N_HEADS, 2 * nq, V_DIM), BF16), pltpu.VMEM((N_HEADS, 1, 2 * nq), F32),
                        pltpu.VMEM((N_HEADS, 1, 2 * nq), F32), pltpu.VMEM((N_HEADS, V_DIM, 2 * nq), F32)],
        compiler_params=pltpu.CompilerParams(dimension_semantics=("arbitrary", "arbitrary"),
                                             vmem_limit_bytes=V7X_VMEM_LIMIT),
        name="attn_sample",
    )(qb, k_cache, v_cache, k_new, v_new, lq, lk, g)


def _mem_kv_kernel(m_ref, g_ref, wck_ref, wcv_ref, mk_ref, mv_ref, mkb_ref, mvb_ref):
    mn = _rmsnorm(m_ref[...], g_ref[...]).astype(BF16)
    k = _dot(mn, wck_ref[...])
    v = _dot(mn, wcv_ref[...])
    _store_heads(mk_ref, k, N_MEM_HEADS, MEM_HEAD_DIM)
    _store_heads(mv_ref, v, N_MEM_HEADS, MEM_HEAD_DIM)
    mkb_ref[...] = k.astype(BF16)
    mvb_ref[...] = v.astype(BF16)


def _mem_kv(mem2d, g, wck, wcv, *, tm=512):
    T = mem2d.shape[0]
    tok = pl.BlockSpec((tm, D_MODEL), lambda i: (i, 0))
    heads = pl.BlockSpec((tm, N_MEM_HEADS, MEM_HEAD_DIM), lambda i: (i, 0, 0))
    f = jax.ShapeDtypeStruct((T, N_MEM_HEADS, MEM_HEAD_DIM), F32)
    b = jax.ShapeDtypeStruct((T, D_MODEL), BF16)
    return pl.pallas_call(
        _mem_kv_kernel,
        grid=(T // tm,),
        in_specs=[tok, _const_spec((1, D_MODEL)), _const_spec(wck.shape), _const_spec(wcv.shape)],
        out_specs=(heads, heads, tok, tok),
        out_shape=(f, f, b, b),
        compiler_params=pltpu.CompilerParams(dimension_semantics=("arbitrary",),
                                             vmem_limit_bytes=V7X_VMEM_LIMIT),
        name="mem_kv",
    )(mem2d, g, wck, wcv)


def _tail_kernel(h_ref, ys_ref, ya_ref, mk_ref, mv_ref, wout_ref, gc_ref, wcq_ref, wco_ref,
                 g2_ref, wgu_ref, wd_ref, gf_ref, y_ref, *, nb, rpb, ff_chunks, final_norm):
    if nb == 1:
        ys = ys_ref[...]
    else:
        ys = jnp.concatenate([ys_ref[:, lb * D_SSM:(lb + 1) * D_SSM] for lb in range(nb)], axis=0)
    h = h_ref[...] + _dot(ys, wout_ref[:D_SSM, :]) + _dot(ya_ref[...], wout_ref[D_SSM:, :])

    qn = _rmsnorm(h, gc_ref[...]).astype(BF16)
    q = (_dot(qn, wcq_ref[...]) * (MEM_HEAD_DIM ** -0.5 * LOG2E)).astype(BF16)
    rows = []
    for lb in range(nb):
        heads = []
        for hh in range(N_MEM_HEADS):
            cols = slice(hh * MEM_HEAD_DIM, (hh + 1) * MEM_HEAD_DIM)
            qh = q[lb * rpb:(lb + 1) * rpb, cols]
            s = lax.dot_general(qh, mk_ref[lb, :, cols], _NT, preferred_element_type=F32)
            p = jnp.exp2(s - jnp.max(s, axis=1, keepdims=True))
            o = _dot(p.astype(BF16), mv_ref[lb, :, cols]) / jnp.sum(p, axis=1, keepdims=True)
            heads.append(o)
        rows.append(jnp.concatenate(heads, axis=1))
    o = rows[0] if nb == 1 else jnp.concatenate(rows, axis=0)
    h = h + _dot(o.astype(BF16), wco_ref[...])

    xn = _rmsnorm(h, g2_ref[...]).astype(BF16)
    x2 = h + 0.5 * _swiglu_half(xn, wgu_ref, wd_ref, ff_chunks)
    y_ref[...] = _rmsnorm(x2, gf_ref[...]) if final_norm else x2


def _tail(h2d, ys_tm, ya2d, mkb, mvb, wout, gc, wcq, wco, g2, wgu, wd, gf, *, n_streams, seq, tm, final_norm,
          ff_chunks=2):
    T = h2d.shape[0]
    if seq >= tm:
        nb, rpb = 1, tm
        nt = seq // tm
        ys_map = lambda i: (i % nt, i // nt)
        mem_map = lambda i: (i // nt, 0, 0)
    else:
        nb, rpb = tm // seq, seq
        ys_map = lambda i: (0, i)
        mem_map = lambda i: (i, 0, 0)
    tok = lambda w: pl.BlockSpec((tm, w), lambda i: (i, 0))
    mem_spec = pl.BlockSpec((nb, N_MEM, D_MODEL), mem_map)
    return pl.pallas_call(
        functools.partial(_tail_kernel, nb=nb, rpb=rpb, ff_chunks=ff_chunks, final_norm=final_norm),
        grid=(T // tm,),
        in_specs=[tok(D_MODEL), pl.BlockSpec((rpb, nb * D_SSM), ys_map), tok(D_ATT), mem_spec, mem_spec,
                  _const_spec(wout.shape), _const_spec((1, D_MODEL)), _const_spec(wcq.shape),
                  _const_spec(wco.shape), _const_spec((1, D_MODEL)), _const_spec(wgu.shape),
                  _const_spec(wd.shape), _const_spec((1, D_MODEL))],
        out_specs=tok(D_MODEL),
        out_shape=jax.ShapeDtypeStruct((T, D_MODEL), F32),
        compiler_params=pltpu.CompilerParams(dimension_semantics=("arbitrary",),
                                             vmem_limit_bytes=V7X_VMEM_LIMIT),
        name="tail",
    )(h2d, ys_tm, ya2d, mkb, mvb, wout, gc, wcq, wco, g2, wgu, wd, gf)


def kernel(x_prompt, x_sample, cache_attn_k, cache_attn_v, state_s5_re, state_s5_im, cache_mem_k, cache_mem_v, mem_prompt, g_ffn1, w_ffn1_gu, w_ffn1_d, g_mix, w_in, ssm_a_re, ssm_a_im, ssm_log_dt, ssm_b_re, ssm_b_im, ssm_c_re, ssm_c_im, ssm_d, w_glu, b_glu, lambda_q, lambda_k, g_subln, w_out, g_mem, g_cross, w_cq, w_ck, w_cv, w_co, g_ffn2, w_ffn2_gu, w_ffn2_d, g_final):
    depth = g_ffn1.shape[0]
    B, L, D = x_prompt.shape
    SB, SL, _ = x_sample.shape
    row = lambda a: a.reshape(1, -1)

    xp = x_prompt.reshape(B * L, D)
    xs = x_sample.reshape(SB * SL, D)
    outs = [[] for _ in range(10)]
    for l in range(depth):
        lam_init = 0.8 - 0.6 * math.exp(-0.3 * l)
        wgu1, wd1, win = w_ffn1_gu[l].astype(BF16), w_ffn1_d[l].astype(BF16), w_in[l].astype(BF16)
        wgu2, wd2 = w_ffn2_gu[l].astype(BF16), w_ffn2_d[l].astype(BF16)
        wout, wcq, wco = w_out[l].astype(BF16), w_cq[l].astype(BF16), w_co[l].astype(BF16)
        wck, wcv, wglu = w_ck[l].astype(BF16), w_cv[l].astype(BF16), w_glu[l].astype(BF16)

        abr, abi, bbr, bbi = _s5_prep(ssm_a_re[l], ssm_a_im[l], ssm_log_dt[l], ssm_b_re[l], ssm_b_im[l])
        ab = jnp.stack([abr.reshape(N_STATE), abi.reshape(N_STATE)])
        bbig = jnp.concatenate([_block_diag_in(bbr), _block_diag_in(bbi)], axis=2).astype(BF16)
        cbig = jnp.concatenate([_block_diag_out(ssm_c_re[l]), -_block_diag_out(ssm_c_im[l])], axis=1).astype(BF16)
        s5_w = (ab, bbig, cbig, row(ssm_d[l]), wglu, row(b_glu[l]))
        att_w = (lambda_q[l], lambda_k[l], row(g_subln[l]))
        tail_w = (wout, row(g_cross[l]), wcq, wco, row(g_ffn2[l]), wgu2, wd2, row(g_final))
        last = l == depth - 1

        hp, up, qp, kp, vp, kpb, vpb = _ffn_mix(xp, row(g_ffn1[l]), wgu1, wd1, row(g_mix[l]), win,
                                                 n_streams=B, seq=L, tm=512, bf16_kv=True)
        zeros = jnp.zeros((B, 2 * N_STATE), F32)
        yp_ssm, hlp = _s5(up.reshape(L * B, D_SSM), zeros, *s5_w, nb=B, seq=L, tc=64, slab=512)
        yp_att = _attn_prompt(qp.reshape(B, L, D_ATT), kpb.reshape(B, L, D_ATT), vpb.reshape(B, L, D_ATT),
                              *att_w, lam_init=lam_init)
        mkp, mvp, mkpb, mvpb = _mem_kv(mem_prompt.reshape(B * N_MEM, D), row(g_mem[l]), wck, wcv)
        xp = _tail(hp, yp_ssm.reshape(L, B * D_SSM), yp_att.reshape(B * L, D_ATT),
                   mkpb.reshape(B, N_MEM, D), mvpb.reshape(B, N_MEM, D), *tail_w,
                   n_streams=B, seq=L, tm=512, final_norm=last)

        hs, us, qs, ks, vs = _ffn_mix(xs, row(g_ffn1[l]), wgu1, wd1, row(g_mix[l]), win,
                                      n_streams=SB, seq=SL, tm=512, bf16_kv=False)
        ks = ks.reshape(SB, SL, N_HEADS, V_DIM)
        vs = vs.reshape(SB, SL, N_HEADS, V_DIM)
        h0 = _state_to_cols(state_s5_re[l], state_s5_im[l])
        ys_ssm, hls = _s5(us.reshape(SL * SB, D_SSM), h0, *s5_w, nb=SB, seq=SL, tc=SL, slab=256)
        key_head_rows = lambda a: a.reshape(SB, -1, V_DIM)
        ys_att = _attn_sample(qs.reshape(SB, SL, D_ATT), key_head_rows(cache_attn_k[l]),
                              key_head_rows(cache_attn_v[l]), key_head_rows(ks), key_head_rows(vs),
                              *att_w, lam_init=lam_init)
        xs = _tail(hs, ys_ssm.reshape(SL, SB * D_SSM), ys_att.reshape(SB * SL, D_ATT),
                   cache_mem_k[l].reshape(SB, N_MEM, D).astype(BF16),
                   cache_mem_v[l].reshape(SB, N_MEM, D).astype(BF16), *tail_w,
                   n_streams=SB, seq=SL, tm=256, final_norm=last)

        rep, imp = _cols_to_state(hlp)
        res, ims = _cols_to_state(hls)
        layer = (kp.reshape(B, L, N_HEADS, V_DIM), vp.reshape(B, L, N_HEADS, V_DIM), rep, imp,
                 mkp.reshape(B, N_MEM, N_MEM_HEADS, MEM_HEAD_DIM), mvp.reshape(B, N_MEM, N_MEM_HEADS, MEM_HEAD_DIM),
                 ks, vs, res, ims)
        for o, a in zip(outs, layer):
            o.append(a)

    return (xp.reshape(B, L, D), xs.reshape(SB, SL, D)) + tuple(jnp.stack(o) for o in outs)
```

```python
import functools
import math

import jax
import jax.numpy as jnp
from jax import lax
from jax.experimental import pallas as pl
from jax.experimental.pallas import tpu as pltpu

F32 = jnp.float32
BF16 = jnp.bfloat16

D_MODEL = 1024
CHUNK = 64
D_SSM = 512
SSM_GROUP = 16
N_SSM_GROUPS = 32
SSM_STATE = 64
D_ATT = 512
N_HEADS = 4
HEAD_DIM = 64
V_DIM = 128
D_IN = D_SSM + 3 * D_ATT
N_MEM = 256
N_MEM_HEADS = 4
MEM_HEAD_DIM = 256
D_FF = 2816
EPS = 1e-6
LOG2E = math.log2(math.e)

N_STATE = N_SSM_GROUPS * SSM_STATE
SSM_BLOCKS = 2
BLK_CH = D_SSM // SSM_BLOCKS
BLK_ST = N_STATE // SSM_BLOCKS

LANES = 128
MXU_DIM = 256
FF_BOUNDS = tuple(range(0, D_FF, 3 * MXU_DIM)) + (D_FF,)
V7X_VMEM_LIMIT = 56 * 1024 * 1024

_NT = (((1,), (1,)), ((), ()))
_TN = (((0,), (0,)), ((), ()))


def _rmsnorm(x, g):
    return x * lax.rsqrt(jnp.mean(x * x, axis=-1, keepdims=True) + EPS) * g


def _dot(a, b):
    return jnp.dot(a, b, preferred_element_type=F32)


def _const_spec(shape):
    nd = len(shape)
    return pl.BlockSpec(shape, lambda *_: (0,) * nd, pipeline_mode=pl.Buffered(1))


def _swiglu_half(xn, wgu_ref, wd_ref):
    def gate_up(c):
        lo, hi = FF_BOUNDS[c], FF_BOUNDS[c + 1]
        return _dot(xn, wgu_ref[:, lo:hi]), _dot(xn, wgu_ref[:, D_FF + lo:D_FF + hi])

    n_chunks = len(FF_BOUNDS) - 1
    acc = None
    nxt = gate_up(0)
    for c in range(n_chunks):
        gate, up = nxt
        if c + 1 < n_chunks:
            nxt = gate_up(c + 1)
        act = (gate * jax.nn.sigmoid(gate) * up).astype(BF16)
        part = _dot(act, wd_ref[FF_BOUNDS[c]:FF_BOUNDS[c + 1], :])
        acc = part if acc is None else acc + part
    return acc


def _store_heads(ref, x, n_heads, width):
    for hh in range(n_heads):
        ref[:, hh, :] = x[:, hh * width:(hh + 1) * width]


def _ffn_mix_kernel(x_ref, g1_ref, wgu_ref, wd_ref, gm_ref, win_ref,
                    h_ref, u_ref, qb_ref, k_ref, v_ref, *kvb_refs):
    x = x_ref[...]
    xn = _rmsnorm(x, g1_ref[...]).astype(BF16)
    h = x + 0.5 * _swiglu_half(xn, wgu_ref, wd_ref)
    h_ref[...] = h
    hn = _rmsnorm(h, gm_ref[...]).astype(BF16)
    z = _dot(hn, win_ref[...])
    u_ref[...] = z[:, :D_SSM]
    q = z[:, D_SSM:D_SSM + D_ATT]
    k = z[:, D_SSM + D_ATT:D_SSM + 2 * D_ATT]
    v = z[:, D_SSM + 2 * D_ATT:]
    qb_ref[...] = (q * (HEAD_DIM ** -0.5 * LOG2E)).astype(BF16)
    _store_heads(k_ref, k, N_HEADS, V_DIM)
    _store_heads(v_ref, v, N_HEADS, V_DIM)
    if kvb_refs:
        kb_ref, vb_ref = kvb_refs
        kb_ref[...] = k.astype(BF16)
        vb_ref[...] = v.astype(BF16)


def _ffn_mix(x2d, g1, wgu, wd, gm, win, *, tm, bf16_kv):
    T = x2d.shape[0]
    tok = lambda w: pl.BlockSpec((tm, w), lambda i: (i, 0))
    heads = pl.BlockSpec((tm, N_HEADS, V_DIM), lambda i: (i, 0, 0))
    out_shape = [
        jax.ShapeDtypeStruct((T, D_MODEL), F32),
        jax.ShapeDtypeStruct((T, D_SSM), F32),
        jax.ShapeDtypeStruct((T, D_ATT), BF16),
        jax.ShapeDtypeStruct((T, N_HEADS, V_DIM), F32),
        jax.ShapeDtypeStruct((T, N_HEADS, V_DIM), F32),
    ]
    out_specs = [tok(D_MODEL), tok(D_SSM), tok(D_ATT), heads, heads]
    if bf16_kv:
        out_shape += [jax.ShapeDtypeStruct((T, D_ATT), BF16)] * 2
        out_specs += [tok(D_ATT)] * 2
    return pl.pallas_call(
        _ffn_mix_kernel,
        grid=(T // tm,),
        in_specs=[tok(D_MODEL), _const_spec((1, D_MODEL)), _const_spec(wgu.shape), _const_spec(wd.shape),
                  _const_spec((1, D_MODEL)), _const_spec(win.shape)],
        out_specs=tuple(out_specs),
        out_shape=tuple(out_shape),
        compiler_params=pltpu.CompilerParams(dimension_semantics=("arbitrary",),
                                             vmem_limit_bytes=V7X_VMEM_LIMIT),
        name="ffn_mix",
    )(x2d, g1, wgu, wd, gm, win)


def _s5_prep_kernel(are_ref, aim_ref, ldt_ref, bre_ref, bim_ref, abr_ref, abi_ref, bbr_ref, bbi_ref):
    lre = are_ref[...]
    lim = aim_ref[...]
    dt = jnp.exp(ldt_ref[...])
    mag = jnp.exp(lre * dt)
    ar = mag * jnp.cos(lim * dt)
    ai = mag * jnp.sin(lim * dt)
    abr_ref[...] = ar
    abi_ref[...] = ai
    den = lre * lre + lim * lim
    xr = ar - 1.0
    cr = (xr * lre + ai * lim) / den
    ci = (ai * lre - xr * lim) / den
    br = bre_ref[...]
    bi = bim_ref[...]
    bbr_ref[...] = cr * br - ci * bi
    bbi_ref[...] = cr * bi + ci * br


def _s5_prep(a_re, a_im, log_dt, b_re, b_im):
    G, P, H = N_SSM_GROUPS, SSM_STATE, SSM_GROUP
    per_row = lambda a: jnp.repeat(a, H, axis=0)
    b_re_t = jnp.swapaxes(b_re, 1, 2).reshape(G * H, P)
    b_im_t = jnp.swapaxes(b_im, 1, 2).reshape(G * H, P)
    o = jax.ShapeDtypeStruct((G * H, P), F32)
    abr, abi, bbr, bbi = pl.pallas_call(
        _s5_prep_kernel,
        out_shape=(o, o, o, o),
        name="s5_prep",
    )(per_row(a_re), per_row(a_im), per_row(log_dt.reshape(G, 1)), b_re_t, b_im_t)
    return (abr.reshape(G, H, P)[:, 0], abi.reshape(G, H, P)[:, 0], bbr.reshape(G, H, P), bbi.reshape(G, H, P))


def _block_diag_in(w_ghp):
    gpb = N_SSM_GROUPS // SSM_BLOCKS
    w = w_ghp.reshape(SSM_BLOCKS, gpb, SSM_GROUP, SSM_STATE)
    eye = jnp.eye(gpb, dtype=w.dtype)
    return jnp.einsum('jghp,gk->jghkp', w, eye).reshape(SSM_BLOCKS, BLK_CH, BLK_ST)


def _block_diag_out(w_ghp):
    gpb = N_SSM_GROUPS // SSM_BLOCKS
    w = w_ghp.reshape(SSM_BLOCKS, gpb, SSM_GROUP, SSM_STATE)
    eye = jnp.eye(gpb, dtype=w.dtype)
    return jnp.einsum('jghp,gk->jgpkh', w, eye).reshape(SSM_BLOCKS, BLK_ST, BLK_CH)


def _gelu_tanh(x):
    return 0.5 * x * (1.0 + jnp.tanh(math.sqrt(2.0 / math.pi) * (x + 0.044715 * (x * x * x))))


def _s5_kernel(u_ref, h0_ref, ab_ref, bbig_ref, cbig_ref, d_ref, wglu_ref, bglu_ref,
               y_ref, hl_ref, hst, abr, abi, io_sc, bu_sc, *, nb, tc, slab):
    c = pl.program_id(0)
    n_lt = D_SSM // LANES

    @pl.when(c == 0)
    def _():
        hst[...] = h0_ref[...]
        abr[...] = jnp.broadcast_to(ab_ref[0:1, :], (nb, N_STATE))
        abi[...] = jnp.broadcast_to(ab_ref[1:2, :], (nb, N_STATE))

    for b in range(nb):
        for lt in range(n_lt):
            io_sc[lt, pl.ds(b, tc, stride=nb), :] = u_ref[b, :, lt * LANES:(lt + 1) * LANES]
    u = jnp.concatenate([io_sc[lt] for lt in range(n_lt)], axis=1)
    ub = u.astype(BF16)
    for j in range(SSM_BLOCKS):
        bu_sc[j] = _dot(ub[:, BLK_CH * j:BLK_CH * (j + 1)], bbig_ref[j])
    ys = []
    for j in range(SSM_BLOCKS):
        for s in range(BLK_ST // slab):
            cr = slice(s * slab, (s + 1) * slab)
            ci = slice(BLK_ST + s * slab, BLK_ST + (s + 1) * slab)
            ar = abr[:, BLK_ST * j + s * slab:BLK_ST * j + (s + 1) * slab]
            ai = abi[:, BLK_ST * j + s * slab:BLK_ST * j + (s + 1) * slab]
            sr = slice(2 * BLK_ST * j + s * slab, 2 * BLK_ST * j + (s + 1) * slab)
            si = slice(2 * BLK_ST * j + BLK_ST + s * slab, 2 * BLK_ST * j + BLK_ST + (s + 1) * slab)
            hr, hi = hst[:, sr], hst[:, si]
            for t in range(tc):
                rows = slice(t * nb, (t + 1) * nb)
                hr, hi = (ar * hr - ai * hi + bu_sc[j, rows, cr], ar * hi + ai * hr + bu_sc[j, rows, ci])
                bu_sc[j, rows, cr] = hr
                bu_sc[j, rows, ci] = hi
            hst[:, sr] = hr
            hst[:, si] = hi
        ys.append(_dot(bu_sc[j].astype(BF16), cbig_ref[j]))
    y = jnp.concatenate(ys, axis=1) + d_ref[...] * u
    y = _gelu_tanh(y)
    gate = _dot(y.astype(BF16), wglu_ref[...]) + bglu_ref[...]
    out = y * jax.nn.sigmoid(gate)
    for lt in range(n_lt):
        io_sc[lt] = out[:, lt * LANES:(lt + 1) * LANES]
    for b in range(nb):
        for lt in range(n_lt):
            y_ref[b, :, lt * LANES:(lt + 1) * LANES] = io_sc[lt, pl.ds(b, tc, stride=nb), :].astype(BF16)

    @pl.when(c == pl.num_programs(0) - 1)
    def _():
        hl_ref[...] = hst[...]


def _s5(u, h0, ab, bbig, cbig, d, wglu, bglu, *, tc, slab):
    nb, seq, _ = u.shape
    rows = tc * nb
    io_spec = pl.BlockSpec((nb, tc, D_SSM), lambda c: (0, c, 0))
    return pl.pallas_call(
        functools.partial(_s5_kernel, nb=nb, tc=tc, slab=slab),
        grid=(seq // tc,),
        in_specs=[io_spec,
                  _const_spec((nb, 2 * N_STATE)), _const_spec((2, N_STATE)),
                  _const_spec(bbig.shape), _const_spec(cbig.shape),
                  _const_spec((1, D_SSM)), _const_spec((D_SSM, D_SSM)), _const_spec((1, D_SSM))],
        out_specs=(io_spec, pl.BlockSpec((nb, 2 * N_STATE), lambda c: (0, 0))),
        out_shape=(jax.ShapeDtypeStruct((nb, seq, D_SSM), BF16),
                   jax.ShapeDtypeStruct((nb, 2 * N_STATE), F32)),
        scratch_shapes=[pltpu.VMEM((nb, 2 * N_STATE), F32), pltpu.VMEM((nb, N_STATE), F32),
                        pltpu.VMEM((nb, N_STATE), F32), pltpu.VMEM((D_SSM // LANES, rows, LANES), F32),
                        pltpu.VMEM((SSM_BLOCKS, rows, 2 * BLK_ST), F32)],
        compiler_params=pltpu.CompilerParams(dimension_semantics=("arbitrary",),
                                             vmem_limit_bytes=V7X_VMEM_LIMIT),
        name="s5_scan",
    )(u, h0, ab, bbig, cbig, d, wglu, bglu)


def _state_to_cols(s_re, s_im):
    nb = s_re.shape[0]
    r = s_re.reshape(nb, SSM_BLOCKS, 1, BLK_ST)
    i = s_im.reshape(nb, SSM_BLOCKS, 1, BLK_ST)
    return jnp.concatenate([r, i], axis=2).reshape(nb, 2 * N_STATE)


def _cols_to_state(h):
    nb = h.shape[0]
    h4 = h.reshape(nb, SSM_BLOCKS, 2, BLK_ST)
    return (h4[:, :, 0].reshape(nb, N_SSM_GROUPS, SSM_STATE), h4[:, :, 1].reshape(nb, N_SSM_GROUPS, SSM_STATE))


def _two_map_queries(q):
    lane = lax.broadcasted_iota(jnp.int32, q.shape, 1)
    lo = lane < HEAD_DIM
    zero = jnp.zeros_like(q)
    return jnp.concatenate([jnp.where(lo, q, zero), jnp.where(lo, zero, q)], axis=0)


def _diff_lambda(lq_ref, lk_ref, lam_init):
    p = lq_ref[...] * lk_ref[...]
    s = jnp.sum(p, axis=1, keepdims=True)
    e = jnp.exp(s)
    return e[0:1, :] - e[1:2, :] + lam_init


def _softmax_update(s_tiles, shift, m_sc, l_sc):
    m_in = m_sc[...] - shift
    mx = jnp.concatenate([jnp.max(t, axis=0, keepdims=True) for t in s_tiles], axis=1)
    m_new = jnp.maximum(m_in, mx)
    alpha = jnp.exp2(m_in - m_new)
    p_tiles = [jnp.exp2(t - m_new[:, LANES * n:LANES * (n + 1)]) for n, t in enumerate(s_tiles)]
    l_sc[...] = alpha * l_sc[...] + jnp.concatenate([jnp.sum(p, axis=0, keepdims=True) for p in p_tiles], axis=1)
    m_sc[...] = m_new
    return jnp.concatenate(p_tiles, axis=1).astype(BF16), alpha


def _accumulate_values(vb, p, alpha, acc_sc):
    acc_sc[...] = alpha * acc_sc[...] + lax.dot_general(vb, p, _TN, preferred_element_type=F32)


def _softmax_block(s_tiles, vb, shift, m_sc, l_sc, acc_sc):
    p, alpha = _softmax_update(s_tiles, shift, m_sc, l_sc)
    _accumulate_values(vb, p, alpha, acc_sc)


def _diff_finish(acc_t, l, n, lam, g, lam_init):
    o = (acc_t * (1.0 / l)).T
    return _rmsnorm(o[:n] - lam * o[n:], g) * (1.0 - lam_init)


def _in_block_bias(n_keys, n_queries, slope2):
    kl = lax.broadcasted_iota(jnp.int32, (n_keys, n_queries), 0)
    ql = lax.broadcasted_iota(jnp.int32, (n_keys, n_queries), 1)
    return kl, ql, slope2 * (ql - jnp.abs(ql - kl)).astype(F32)


def _attn_prompt_kernel(q_ref, k_ref, v_ref, lq_ref, lk_ref, g_ref, o_ref,
                        qq_sc, cb_sc, db_sc, m_sc, l_sc, acc_sc, *, bq, lam_init):
    i = pl.program_id(1)
    slopes2 = [2.0 ** (-8.0 * (h + 1) / N_HEADS) * LOG2E for h in range(N_HEADS)]

    @pl.when((pl.program_id(0) == 0) & (i == 0))
    def _():
        krow = lax.broadcasted_iota(jnp.int32, (bq, LANES), 0).astype(F32)
        kl, ql, rel = _in_block_bias(bq, bq, 1.0)
        visible = (kl // CHUNK) <= (ql // CHUNK)
        for h in range(N_HEADS):
            cb_sc[h] = slopes2[h] * krow
            db_sc[h] = jnp.where(visible, slopes2[h] * rel, -jnp.inf)

    for h in range(N_HEADS):
        qq_sc[h] = _two_map_queries(q_ref[0, :, h * V_DIM:(h + 1) * V_DIM])
    m_sc[...] = jnp.full(m_sc.shape, -jnp.inf, F32)
    l_sc[...] = jnp.zeros(l_sc.shape, F32)
    acc_sc[...] = jnp.zeros(acc_sc.shape, F32)
    n_tiles = 2 * bq // LANES
    q_tiles = n_tiles // 2

    def step(j, diag):
        rows = pl.ds(pl.multiple_of(j * bq, bq), bq)
        cols = [slice(h * V_DIM, (h + 1) * V_DIM) for h in range(N_HEADS)]
        scores = [lax.dot_general(k_ref[0, rows, cols[h]], qq_sc[h], _NT, preferred_element_type=F32)
                  for h in range(N_HEADS)]
        probs = []
        for h, s in enumerate(scores):
            if diag:
                tiles = [s[:, LANES * n:LANES * (n + 1)]
                         + db_sc[h, :, LANES * (n % q_tiles):LANES * (n % q_tiles + 1)] for n in range(n_tiles)]
            else:
                tiles = [s[:, LANES * n:LANES * (n + 1)] + cb_sc[h] for n in range(n_tiles)]
            probs.append(_softmax_update(tiles, slopes2[h] * bq, m_sc.at[h], l_sc.at[h]))
        for h, (p, alpha) in enumerate(probs):
            _accumulate_values(v_ref[0, rows, cols[h]], p, alpha, acc_sc.at[h])

    def body(j, carry):
        step(j, False)
        return carry

    lax.fori_loop(0, i, body, 0)
    step(i, True)
    lam = _diff_lambda(lq_ref, lk_ref, lam_init)
    for h in range(N_HEADS):
        o = _diff_finish(acc_sc[h], l_sc[h], bq, lam, g_ref[...], lam_init)
        o_ref[0, :, h * V_DIM:(h + 1) * V_DIM] = o.astype(o_ref.dtype)


def _attn_prompt(qb, kb, vb, lq, lk, g, *, lam_init, bq=256):
    B, L, _ = qb.shape
    small = lambda shape: pl.BlockSpec(shape, lambda b, i: (0, 0))
    return pl.pallas_call(
        functools.partial(_attn_prompt_kernel, bq=bq, lam_init=lam_init),
        grid=(B, L // bq),
        in_specs=[pl.BlockSpec((1, bq, D_ATT), lambda b, i: (b, i, 0)),
                  pl.BlockSpec((1, L, D_ATT), lambda b, i: (b, 0, 0)),
                  pl.BlockSpec((1, L, D_ATT), lambda b, i: (b, 0, 0)),
                  small((2, HEAD_DIM)), small((2, HEAD_DIM)), small((1, V_DIM))],
        out_specs=pl.BlockSpec((1, bq, D_ATT), lambda b, i: (b, i, 0)),
        out_shape=jax.ShapeDtypeStruct((B, L, D_ATT), BF16),
        scratch_shapes=[pltpu.VMEM((N_HEADS, 2 * bq, V_DIM), BF16),
                        pltpu.VMEM((N_HEADS, bq, LANES), F32), pltpu.VMEM((N_HEADS, bq, bq), F32),
                        pltpu.VMEM((N_HEADS, 1, 2 * bq), F32), pltpu.VMEM((N_HEADS, 1, 2 * bq), F32),
                        pltpu.VMEM((N_HEADS, V_DIM, 2 * bq), F32)],
        compiler_params=pltpu.CompilerParams(dimension_semantics=("arbitrary", "arbitrary")),
        name="attn_prompt",
    )(qb, kb, vb, lq, lk, g)


def _attn_sample_kernel(q_ref, kc_ref, vc_ref, kn_ref, vn_ref, lq_ref, lk_ref, g_ref, o_ref,
                        qq_sc, m_sc, l_sc, acc_sc, *, nq, bk, lam_init):
    j = pl.program_id(1)

    @pl.when(j == 0)
    def _():
        for h in range(N_HEADS):
            qq_sc[h] = _two_map_queries(q_ref[0, :, h * V_DIM:(h + 1) * V_DIM])
        m_sc[...] = jnp.full(m_sc.shape, -jnp.inf, F32)
        l_sc[...] = jnp.zeros(l_sc.shape, F32)
        acc_sc[...] = jnp.zeros(acc_sc.shape, F32)

    def block(k_blk_ref, v_blk_ref, n_keys, bias_of):
        for h in range(N_HEADS):
            slope2 = 2.0 ** (-8.0 * (h + 1) / N_HEADS) * LOG2E
            kb = k_blk_ref[0, pl.ds(h, n_keys, stride=N_HEADS), :].astype(BF16)
            vb = v_blk_ref[0, pl.ds(h, n_keys, stride=N_HEADS), :].astype(BF16)
            s = lax.dot_general(kb, qq_sc[h], _NT, preferred_element_type=F32)
            _softmax_block([s + bias_of(slope2)], vb, slope2 * bk, m_sc.at[h], l_sc.at[h], acc_sc.at[h])

    krow = lax.broadcasted_iota(jnp.int32, (bk, 2 * nq), 0).astype(F32)
    block(kc_ref, vc_ref, bk, lambda slope2: slope2 * krow)

    @pl.when(j == pl.num_programs(1) - 1)
    def _():
        kl = lax.broadcasted_iota(jnp.int32, (nq, 2 * nq), 0)
        lane = lax.broadcasted_iota(jnp.int32, (nq, 2 * nq), 1)
        ql = jnp.where(lane >= nq, lane - nq, lane)
        rel = (ql - jnp.abs(ql - kl)).astype(F32)
        block(kn_ref, vn_ref, nq, lambda slope2: slope2 * rel)
        lam = _diff_lambda(lq_ref, lk_ref, lam_init)
        for h in range(N_HEADS):
            o = _diff_finish(acc_sc[h], l_sc[h], nq, lam, g_ref[...], lam_init)
            o_ref[0, :, h * V_DIM:(h + 1) * V_DIM] = o.astype(o_ref.dtype)


def _attn_sample(qb, k_cache, v_cache, k_new, v_new, lq, lk, g, *, lam_init, bk=1024):
    B, nq, _ = qb.shape
    past = k_cache.shape[1] // N_HEADS
    q_spec = pl.BlockSpec((1, nq, D_ATT), lambda b, j: (b, 0, 0))
    new_spec = pl.BlockSpec((1, nq * N_HEADS, V_DIM), lambda b, j: (b, 0, 0))
    cache_spec = pl.BlockSpec((1, bk * N_HEADS, V_DIM), lambda b, j: (b, j, 0))
    small = lambda shape: pl.BlockSpec(shape, lambda b, j: (0, 0))
    return pl.pallas_call(
        functools.partial(_attn_sample_kernel, nq=nq, bk=bk, lam_init=lam_init),
        grid=(B, past // bk),
        in_specs=[q_spec, cache_spec, cache_spec, new_spec, new_spec,
                  small((2, HEAD_DIM)), small((2, HEAD_DIM)), small((1, V_DIM))],
        out_specs=q_spec,
        out_shape=jax.ShapeDtypeStruct((B, nq, D_ATT), BF16),
        scratch_shapes=[pltpu.VMEM((N_HEADS, 2 * nq, V_DIM), BF16), pltpu.VMEM((N_HEADS, 1, 2 * nq), F32),
                        pltpu.VMEM((N_HEADS, 1, 2 * nq), F32), pltpu.VMEM((N_HEADS, V_DIM, 2 * nq), F32)],
        compiler_params=pltpu.CompilerParams(dimension_semantics=("arbitrary", "arbitrary"),
                                             vmem_limit_bytes=V7X_VMEM_LIMIT),
        name="attn_sample",
    )(qb, k_cache, v_cache, k_new, v_new, lq, lk, g)


def _mem_kv_kernel(m_ref, g_ref, wck_ref, wcv_ref, mk_ref, mv_ref, mkb_ref, mvb_ref):
    mn = _rmsnorm(m_ref[...], g_ref[...]).astype(BF16)
    k = _dot(mn, wck_ref[...])
    v = _dot(mn, wcv_ref[...])
    _store_heads(mk_ref, k, N_MEM_HEADS, MEM_HEAD_DIM)
    _store_heads(mv_ref, v, N_MEM_HEADS, MEM_HEAD_DIM)
    mkb_ref[...] = k.astype(BF16)
    mvb_ref[...] = v.astype(BF16)


def _mem_kv(mem2d, g, wck, wcv, *, tm=512):
    T = mem2d.shape[0]
    tok = pl.BlockSpec((tm, D_MODEL), lambda i: (i, 0))
    heads = pl.BlockSpec((tm, N_MEM_HEADS, MEM_HEAD_DIM), lambda i: (i, 0, 0))
    f = jax.ShapeDtypeStruct((T, N_MEM_HEADS, MEM_HEAD_DIM), F32)
    b = jax.ShapeDtypeStruct((T, D_MODEL), BF16)
    return pl.pallas_call(
        _mem_kv_kernel,
        grid=(T // tm,),
        in_specs=[tok, _const_spec((1, D_MODEL)), _const_spec(wck.shape), _const_spec(wcv.shape)],
        out_specs=(heads, heads, tok, tok),
        out_shape=(f, f, b, b),
        compiler_params=pltpu.CompilerParams(dimension_semantics=("arbitrary",),
                                             vmem_limit_bytes=V7X_VMEM_LIMIT),
        name="mem_kv",
    )(mem2d, g, wck, wcv)


def _tail_kernel(h_ref, ys_ref, ya_ref, mk_ref, mv_ref, wout_ref, gc_ref, wcq_ref, wco_ref,
                 g2_ref, wgu_ref, wd_ref, gf_ref, y_ref, *, nb, rpb, final_norm):
    h = h_ref[...] + _dot(ys_ref[...], wout_ref[:D_SSM, :]) + _dot(ya_ref[...], wout_ref[D_SSM:, :])

    qn = _rmsnorm(h, gc_ref[...]).astype(BF16)
    q = (_dot(qn, wcq_ref[...]) * (MEM_HEAD_DIM ** -0.5 * LOG2E)).astype(BF16)
    rows = []
    for lb in range(nb):
        heads = []
        for hh in range(N_MEM_HEADS):
            cols = slice(hh * MEM_HEAD_DIM, (hh + 1) * MEM_HEAD_DIM)
            qh = q[lb * rpb:(lb + 1) * rpb, cols]
            s = lax.dot_general(qh, mk_ref[lb, :, cols], _NT, preferred_element_type=F32)
            p = jnp.exp2(s - jnp.max(s, axis=1, keepdims=True))
            o = _dot(p.astype(BF16), mv_ref[lb, :, cols]) / jnp.sum(p, axis=1, keepdims=True)
            heads.append(o)
        rows.append(jnp.concatenate(heads, axis=1))
    o = rows[0] if nb == 1 else jnp.concatenate(rows, axis=0)
    h = h + _dot(o.astype(BF16), wco_ref[...])

    xn = _rmsnorm(h, g2_ref[...]).astype(BF16)
    x2 = h + 0.5 * _swiglu_half(xn, wgu_ref, wd_ref)
    y_ref[...] = _rmsnorm(x2, gf_ref[...]) if final_norm else x2


def _tail(h2d, ys2d, ya2d, mkb, mvb, wout, gc, wcq, wco, g2, wgu, wd, gf, *, seq, tm, final_norm):
    T = h2d.shape[0]
    if seq >= tm:
        nb, rpb = 1, tm
        nt = seq // tm
        mem_map = lambda i: (i // nt, 0, 0)
    else:
        nb, rpb = tm // seq, seq
        mem_map = lambda i: (i, 0, 0)
    tok = lambda w: pl.BlockSpec((tm, w), lambda i: (i, 0))
    mem_spec = pl.BlockSpec((nb, N_MEM, D_MODEL), mem_map)
    return pl.pallas_call(
        functools.partial(_tail_kernel, nb=nb, rpb=rpb, final_norm=final_norm),
        grid=(T // tm,),
        in_specs=[tok(D_MODEL), tok(D_SSM), tok(D_ATT), mem_spec, mem_spec,
                  _const_spec(wout.shape), _const_spec((1, D_MODEL)), _const_spec(wcq.shape),
                  _const_spec(wco.shape), _const_spec((1, D_MODEL)), _const_spec(wgu.shape),
                  _const_spec(wd.shape), _const_spec((1, D_MODEL))],
        out_specs=tok(D_MODEL),
        out_shape=jax.ShapeDtypeStruct((T, D_MODEL), F32),
        compiler_params=pltpu.CompilerParams(dimension_semantics=("arbitrary",),
                                             vmem_limit_bytes=V7X_VMEM_LIMIT),
        name="tail",
    )(h2d, ys2d, ya2d, mkb, mvb, wout, gc, wcq, wco, g2, wgu, wd, gf)


def kernel(x_prompt, x_sample, cache_attn_k, cache_attn_v, state_s5_re, state_s5_im, cache_mem_k, cache_mem_v, mem_prompt, g_ffn1, w_ffn1_gu, w_ffn1_d, g_mix, w_in, ssm_a_re, ssm_a_im, ssm_log_dt, ssm_b_re, ssm_b_im, ssm_c_re, ssm_c_im, ssm_d, w_glu, b_glu, lambda_q, lambda_k, g_subln, w_out, g_mem, g_cross, w_cq, w_ck, w_cv, w_co, g_ffn2, w_ffn2_gu, w_ffn2_d, g_final):
    depth = g_ffn1.shape[0]
    B, L, D = x_prompt.shape
    SB, SL, _ = x_sample.shape
    row = lambda a: a.reshape(1, -1)

    xp = x_prompt.reshape(B * L, D)
    xs = x_sample.reshape(SB * SL, D)
    outs = [[] for _ in range(10)]
    for l in range(depth):
        lam_init = 0.8 - 0.6 * math.exp(-0.3 * l)
        wgu1, wd1, win = w_ffn1_gu[l].astype(BF16), w_ffn1_d[l].astype(BF16), w_in[l].astype(BF16)
        wgu2, wd2 = w_ffn2_gu[l].astype(BF16), w_ffn2_d[l].astype(BF16)
        wout, wcq, wco = w_out[l].astype(BF16), w_cq[l].astype(BF16), w_co[l].astype(BF16)
        wck, wcv, wglu = w_ck[l].astype(BF16), w_cv[l].astype(BF16), w_glu[l].astype(BF16)

        abr, abi, bbr, bbi = _s5_prep(ssm_a_re[l], ssm_a_im[l], ssm_log_dt[l], ssm_b_re[l], ssm_b_im[l])
        ab = jnp.stack([abr.reshape(N_STATE), abi.reshape(N_STATE)])
        bbig = jnp.concatenate([_block_diag_in(bbr), _block_diag_in(bbi)], axis=2).astype(BF16)
        cbig = jnp.concatenate([_block_diag_out(ssm_c_re[l]), -_block_diag_out(ssm_c_im[l])], axis=1).astype(BF16)
        s5_w = (ab, bbig, cbig, row(ssm_d[l]), wglu, row(b_glu[l]))
        att_w = (lambda_q[l], lambda_k[l], row(g_subln[l]))
        tail_w = (wout, row(g_cross[l]), wcq, wco, row(g_ffn2[l]), wgu2, wd2, row(g_final))
        last = l == depth - 1

        hp, up, qp, kp, vp, kpb, vpb = _ffn_mix(xp, row(g_ffn1[l]), wgu1, wd1, row(g_mix[l]), win,
                                                 tm=512, bf16_kv=True)
        zeros = jnp.zeros((B, 2 * N_STATE), F32)
        yp_ssm, hlp = _s5(up.reshape(B, L, D_SSM), zeros, *s5_w, tc=CHUNK, slab=512)
        yp_att = _attn_prompt(qp.reshape(B, L, D_ATT), kpb.reshape(B, L, D_ATT), vpb.reshape(B, L, D_ATT),
                              *att_w, lam_init=lam_init)
        mkp, mvp, mkpb, mvpb = _mem_kv(mem_prompt.reshape(B * N_MEM, D), row(g_mem[l]), wck, wcv)
        xp = _tail(hp, yp_ssm.reshape(B * L, D_SSM), yp_att.reshape(B * L, D_ATT),
                   mkpb.reshape(B, N_MEM, D), mvpb.reshape(B, N_MEM, D), *tail_w,
                   seq=L, tm=512, final_norm=last)

        hs, us, qs, ks, vs = _ffn_mix(xs, row(g_ffn1[l]), wgu1, wd1, row(g_mix[l]), win,
                                      tm=512, bf16_kv=False)
        ks = ks.reshape(SB, SL, N_HEADS, V_DIM)
        vs = vs.reshape(SB, SL, N_HEADS, V_DIM)
        h0 = _state_to_cols(state_s5_re[l], state_s5_im[l])
        ys_ssm, hls = _s5(us.reshape(SB, SL, D_SSM), h0, *s5_w, tc=SL, slab=256)
        key_head_rows = lambda a: a.reshape(SB, -1, V_DIM)
        ys_att = _attn_sample(qs.reshape(SB, SL, D_ATT), key_head_rows(cache_attn_k[l]),
                              key_head_rows(cache_attn_v[l]), key_head_rows(ks), key_head_rows(vs),
                              *att_w, lam_init=lam_init)
        xs = _tail(hs, ys_ssm.reshape(SB * SL, D_SSM), ys_att.reshape(SB * SL, D_ATT),
                   cache_mem_k[l].reshape(SB, N_MEM, D).astype(BF16),
                   cache_mem_v[l].reshape(SB, N_MEM, D).astype(BF16), *tail_w,
                   seq=SL, tm=256, final_norm=last)

        rep, imp = _cols_to_state(hlp)
        res, ims = _cols_to_state(hls)
        layer = (kp.reshape(B, L, N_HEADS, V_DIM), vp.reshape(B, L, N_HEADS, V_DIM), rep, imp,
                 mkp.reshape(B, N_MEM, N_MEM_HEADS, MEM_HEAD_DIM), mvp.reshape(B, N_MEM, N_MEM_HEADS, MEM_HEAD_DIM),
                 ks, vs, res, ims)
        for o, a in zip(outs, layer):
            o.append(a)

    return (xp.reshape(B, L, D), xs.reshape(SB, SL, D)) + tuple(jnp.stack(o) for o in outs)
```

```python
import functools
import math

import jax
import jax.numpy as jnp
from jax import lax
from jax.experimental import pallas as pl
from jax.experimental.pallas import tpu as pltpu

F32 = jnp.float32
BF16 = jnp.bfloat16

D_MODEL = 1024
CHUNK = 64
D_SSM = 512
SSM_GROUP = 16
N_SSM_GROUPS = 32
SSM_STATE = 64
D_ATT = 512
N_HEADS = 4
HEAD_DIM = 64
V_DIM = 128
D_IN = D_SSM + 3 * D_ATT
N_MEM = 256
N_MEM_HEADS = 4
MEM_HEAD_DIM = 256
D_FF = 2816
EPS = 1e-6
LOG2E = math.log2(math.e)

N_STATE = N_SSM_GROUPS * SSM_STATE
SSM_BLOCKS = 2
BLK_CH = D_SSM // SSM_BLOCKS
BLK_ST = N_STATE // SSM_BLOCKS

LANES = 128
MXU_DIM = 256
FF_BOUNDS = tuple(range(0, D_FF, 3 * MXU_DIM)) + (D_FF,)
V7X_VMEM_LIMIT = 56 * 1024 * 1024

_NT = (((1,), (1,)), ((), ()))
_TN = (((0,), (0,)), ((), ()))


def _rmsnorm(x, g):
    return x * lax.rsqrt(jnp.mean(x * x, axis=-1, keepdims=True) + EPS) * g


def _dot(a, b):
    return jnp.dot(a, b, preferred_element_type=F32)


def _const_spec(shape):
    nd = len(shape)
    return pl.BlockSpec(shape, lambda *_: (0,) * nd, pipeline_mode=pl.Buffered(1))


def _swiglu_half(xn, wgu_ref, wd_ref):
    def gate_up(c):
        lo, hi = FF_BOUNDS[c], FF_BOUNDS[c + 1]
        return _dot(xn, wgu_ref[:, lo:hi]), _dot(xn, wgu_ref[:, D_FF + lo:D_FF + hi])

    n_chunks = len(FF_BOUNDS) - 1
    acc = None
    nxt = gate_up(0)
    for c in range(n_chunks):
        gate, up = nxt
        if c + 1 < n_chunks:
            nxt = gate_up(c + 1)
        act = (gate * jax.nn.sigmoid(gate) * up).astype(BF16)
        part = _dot(act, wd_ref[FF_BOUNDS[c]:FF_BOUNDS[c + 1], :])
        acc = part if acc is None else acc + part
    return acc


def _store_heads(ref, x, n_heads, width):
    for hh in range(n_heads):
        ref[:, hh, :] = x[:, hh * width:(hh + 1) * width]


def _ffn_mix_kernel(x_ref, g1_ref, wgu_ref, wd_ref, gm_ref, win_ref,
                    h_ref, u_ref, qb_ref, k_ref, v_ref, *kvb_refs):
    x = x_ref[...]
    xn = _rmsnorm(x, g1_ref[...]).astype(BF16)
    h = x + 0.5 * _swiglu_half(xn, wgu_ref, wd_ref)
    h_ref[...] = h
    hn = _rmsnorm(h, gm_ref[...]).astype(BF16)
    z = _dot(hn, win_ref[...])
    u_ref[...] = z[:, :D_SSM]
    q = z[:, D_SSM:D_SSM + D_ATT]
    k = z[:, D_SSM + D_ATT:D_SSM + 2 * D_ATT]
    v = z[:, D_SSM + 2 * D_ATT:]
    qb_ref[...] = (q * (HEAD_DIM ** -0.5 * LOG2E)).astype(BF16)
    _store_heads(k_ref, k, N_HEADS, V_DIM)
    _store_heads(v_ref, v, N_HEADS, V_DIM)
    if kvb_refs:
        kb_ref, vb_ref = kvb_refs
        kb_ref[...] = k.astype(BF16)
        vb_ref[...] = v.astype(BF16)


def _ffn_mix(x2d, g1, wgu, wd, gm, win, *, tm, bf16_kv):
    T = x2d.shape[0]
    tok = lambda w: pl.BlockSpec((tm, w), lambda i: (i, 0))
    heads = pl.BlockSpec((tm, N_HEADS, V_DIM), lambda i: (i, 0, 0))
    out_shape = [
        jax.ShapeDtypeStruct((T, D_MODEL), F32),
        jax.ShapeDtypeStruct((T, D_SSM), F32),
        jax.ShapeDtypeStruct((T, D_ATT), BF16),
        jax.ShapeDtypeStruct((T, N_HEADS, V_DIM), F32),
        jax.ShapeDtypeStruct((T, N_HEADS, V_DIM), F32),
    ]
    out_specs = [tok(D_MODEL), tok(D_SSM), tok(D_ATT), heads, heads]
    if bf16_kv:
        out_shape += [jax.ShapeDtypeStruct((T, D_ATT), BF16)] * 2
        out_specs += [tok(D_ATT)] * 2
    return pl.pallas_call(
        _ffn_mix_kernel,
        grid=(T // tm,),
        in_specs=[tok(D_MODEL), _const_spec((1, D_MODEL)), _const_spec(wgu.shape), _const_spec(wd.shape),
                  _const_spec((1, D_MODEL)), _const_spec(win.shape)],
        out_specs=tuple(out_specs),
        out_shape=tuple(out_shape),
        compiler_params=pltpu.CompilerParams(dimension_semantics=("arbitrary",),
                                             vmem_limit_bytes=V7X_VMEM_LIMIT),
        name="ffn_mix",
    )(x2d, g1, wgu, wd, gm, win)


def _s5_prep_kernel(are_ref, aim_ref, ldt_ref, bre_ref, bim_ref, abr_ref, abi_ref, bbr_ref, bbi_ref):
    lre = are_ref[...]
    lim = aim_ref[...]
    dt = jnp.exp(ldt_ref[...])
    mag = jnp.exp(lre * dt)
    ar = mag * jnp.cos(lim * dt)
    ai = mag * jnp.sin(lim * dt)
    abr_ref[...] = ar
    abi_ref[...] = ai
    den = lre * lre + lim * lim
    xr = ar - 1.0
    cr = (xr * lre + ai * lim) / den
    ci = (ai * lre - xr * lim) / den
    br = bre_ref[...]
    bi = bim_ref[...]
    bbr_ref[...] = cr * br - ci * bi
    bbi_ref[...] = cr * bi + ci * br


def _s5_prep(a_re, a_im, log_dt, b_re, b_im):
    G, P, H = N_SSM_GROUPS, SSM_STATE, SSM_GROUP
    per_row = lambda a: jnp.repeat(a, H, axis=0)
    b_re_t = jnp.swapaxes(b_re, 1, 2).reshape(G * H, P)
    b_im_t = jnp.swapaxes(b_im, 1, 2).reshape(G * H, P)
    o = jax.ShapeDtypeStruct((G * H, P), F32)
    abr, abi, bbr, bbi = pl.pallas_call(
        _s5_prep_kernel,
        out_shape=(o, o, o, o),
        name="s5_prep",
    )(per_row(a_re), per_row(a_im), per_row(log_dt.reshape(G, 1)), b_re_t, b_im_t)
    return (abr.reshape(G, H, P)[:, 0], abi.reshape(G, H, P)[:, 0], bbr.reshape(G, H, P), bbi.reshape(G, H, P))


def _block_diag_in(w_ghp):
    gpb = N_SSM_GROUPS // SSM_BLOCKS
    w = w_ghp.reshape(SSM_BLOCKS, gpb, SSM_GROUP, SSM_STATE)
    eye = jnp.eye(gpb, dtype=w.dtype)
    return jnp.einsum('jghp,gk->jghkp', w, eye).reshape(SSM_BLOCKS, BLK_CH, BLK_ST)


def _block_diag_out(w_ghp):
    gpb = N_SSM_GROUPS // SSM_BLOCKS
    w = w_ghp.reshape(SSM_BLOCKS, gpb, SSM_GROUP, SSM_STATE)
    eye = jnp.eye(gpb, dtype=w.dtype)
    return jnp.einsum('jghp,gk->jgpkh', w, eye).reshape(SSM_BLOCKS, BLK_ST, BLK_CH)


def _gelu_tanh(x):
    return 0.5 * x * (1.0 + jnp.tanh(math.sqrt(2.0 / math.pi) * (x + 0.044715 * (x * x * x))))


def _s5_kernel(u_ref, h0_ref, ab_ref, bbig_ref, cbig_ref, d_ref, wglu_ref, bglu_ref,
               y_ref, hl_ref, hst, abr, abi, io_sc, bu_sc, *, nb, tc, slab):
    c = pl.program_id(0)
    n_lt = D_SSM // LANES

    @pl.when(c == 0)
    def _():
        hst[...] = h0_ref[...]
        abr[...] = jnp.broadcast_to(ab_ref[0:1, :], (nb, N_STATE))
        abi[...] = jnp.broadcast_to(ab_ref[1:2, :], (nb, N_STATE))

    for b in range(nb):
        for lt in range(n_lt):
            io_sc[lt, pl.ds(b, tc, stride=nb), :] = u_ref[b, :, lt * LANES:(lt + 1) * LANES]
    u = jnp.concatenate([io_sc[lt] for lt in range(n_lt)], axis=1)
    ub = u.astype(BF16)
    for j in range(SSM_BLOCKS):
        bu_sc[j] = _dot(ub[:, BLK_CH * j:BLK_CH * (j + 1)], bbig_ref[j])
    ys = []
    for j in range(SSM_BLOCKS):
        for s in range(BLK_ST // slab):
            cr = slice(s * slab, (s + 1) * slab)
            ci = slice(BLK_ST + s * slab, BLK_ST + (s + 1) * slab)
            ar = abr[:, BLK_ST * j + s * slab:BLK_ST * j + (s + 1) * slab]
            ai = abi[:, BLK_ST * j + s * slab:BLK_ST * j + (s + 1) * slab]
            sr = slice(2 * BLK_ST * j + s * slab, 2 * BLK_ST * j + (s + 1) * slab)
            si = slice(2 * BLK_ST * j + BLK_ST + s * slab, 2 * BLK_ST * j + BLK_ST + (s + 1) * slab)
            hr, hi = hst[:, sr], hst[:, si]
            for t in range(tc):
                rows = slice(t * nb, (t + 1) * nb)
                hr, hi = (ar * hr - ai * hi + bu_sc[j, rows, cr], ar * hi + ai * hr + bu_sc[j, rows, ci])
                bu_sc[j, rows, cr] = hr
                bu_sc[j, rows, ci] = hi
            hst[:, sr] = hr
            hst[:, si] = hi
        ys.append(_dot(bu_sc[j].astype(BF16), cbig_ref[j]))
    y = jnp.concatenate(ys, axis=1) + d_ref[...] * u
    y = _gelu_tanh(y)
    gate = _dot(y.astype(BF16), wglu_ref[...]) + bglu_ref[...]
    out = y * jax.nn.sigmoid(gate)
    for lt in range(n_lt):
        io_sc[lt] = out[:, lt * LANES:(lt + 1) * LANES]
    for b in range(nb):
        for lt in range(n_lt):
            y_ref[b, :, lt * LANES:(lt + 1) * LANES] = io_sc[lt, pl.ds(b, tc, stride=nb), :].astype(BF16)

    @pl.when(c == pl.num_programs(0) - 1)
    def _():
        hl_ref[...] = hst[...]


def _s5(u, h0, ab, bbig, cbig, d, wglu, bglu, *, tc, slab):
    nb, seq, _ = u.shape
    rows = tc * nb
    io_spec = pl.BlockSpec((nb, tc, D_SSM), lambda c: (0, c, 0))
    return pl.pallas_call(
        functools.partial(_s5_kernel, nb=nb, tc=tc, slab=slab),
        grid=(seq // tc,),
        in_specs=[io_spec,
                  _const_spec((nb, 2 * N_STATE)), _const_spec((2, N_STATE)),
                  _const_spec(bbig.shape), _const_spec(cbig.shape),
                  _const_spec((1, D_SSM)), _const_spec((D_SSM, D_SSM)), _const_spec((1, D_SSM))],
        out_specs=(io_spec, pl.BlockSpec((nb, 2 * N_STATE), lambda c: (0, 0))),
        out_shape=(jax.ShapeDtypeStruct((nb, seq, D_SSM), BF16),
                   jax.ShapeDtypeStruct((nb, 2 * N_STATE), F32)),
        scratch_shapes=[pltpu.VMEM((nb, 2 * N_STATE), F32), pltpu.VMEM((nb, N_STATE), F32),
                        pltpu.VMEM((nb, N_STATE), F32), pltpu.VMEM((D_SSM // LANES, rows, LANES), F32),
                        pltpu.VMEM((SSM_BLOCKS, rows, 2 * BLK_ST), F32)],
        compiler_params=pltpu.CompilerParams(dimension_semantics=("arbitrary",),
                                             vmem_limit_bytes=V7X_VMEM_LIMIT),
        name="s5_scan",
    )(u, h0, ab, bbig, cbig, d, wglu, bglu)


def _state_to_cols(s_re, s_im):
    nb = s_re.shape[0]
    r = s_re.reshape(nb, SSM_BLOCKS, 1, BLK_ST)
    i = s_im.reshape(nb, SSM_BLOCKS, 1, BLK_ST)
    return jnp.concatenate([r, i], axis=2).reshape(nb, 2 * N_STATE)


def _cols_to_state(h):
    nb = h.shape[0]
    h4 = h.reshape(nb, SSM_BLOCKS, 2, BLK_ST)
    return (h4[:, :, 0].reshape(nb, N_SSM_GROUPS, SSM_STATE), h4[:, :, 1].reshape(nb, N_SSM_GROUPS, SSM_STATE))


def _two_map_queries(q):
    lane = lax.broadcasted_iota(jnp.int32, q.shape, 1)
    lo = lane < HEAD_DIM
    zero = jnp.zeros_like(q)
    return jnp.concatenate([jnp.where(lo, q, zero), jnp.where(lo, zero, q)], axis=0)


def _diff_lambda(lq_ref, lk_ref, lam_init):
    p = lq_ref[...] * lk_ref[...]
    s = jnp.sum(p, axis=1, keepdims=True)
    e = jnp.exp(s)
    return e[0:1, :] - e[1:2, :] + lam_init


def _softmax_update(s_tiles, shift, m_sc, l_sc):
    m_in = m_sc[...] - shift
    mx = jnp.concatenate([jnp.max(t, axis=0, keepdims=True) for t in s_tiles], axis=1)
    m_new = jnp.maximum(m_in, mx)
    alpha = jnp.exp2(m_in - m_new)
    p_tiles = [jnp.exp2(t - m_new[:, LANES * n:LANES * (n + 1)]) for n, t in enumerate(s_tiles)]
    l_sc[...] = alpha * l_sc[...] + jnp.concatenate([jnp.sum(p, axis=0, keepdims=True) for p in p_tiles], axis=1)
    m_sc[...] = m_new
    return jnp.concatenate(p_tiles, axis=1).astype(BF16), alpha


def _accumulate_values(vb, p, alpha, acc_sc):
    acc_sc[...] = alpha * acc_sc[...] + lax.dot_general(vb, p, _TN, preferred_element_type=F32)


def _softmax_block(s_tiles, vb, shift, m_sc, l_sc, acc_sc):
    p, alpha = _softmax_update(s_tiles, shift, m_sc, l_sc)
    _accumulate_values(vb, p, alpha, acc_sc)


def _diff_finish(acc_t, l, n, lam, g, lam_init):
    o = (acc_t * (1.0 / l)).T
    return _rmsnorm(o[:n] - lam * o[n:], g) * (1.0 - lam_init)


def _in_block_bias(n_keys, n_queries, slope2):
    kl = lax.broadcasted_iota(jnp.int32, (n_keys, n_queries), 0)
    ql = lax.broadcasted_iota(jnp.int32, (n_keys, n_queries), 1)
    return kl, ql, slope2 * (ql - jnp.abs(ql - kl)).astype(F32)


def _attn_prompt_kernel(q_lo_ref, q_hi_ref, k_ref, v_ref, lq_ref, lk_ref, g_ref, o_ref,
                        qq_sc, cb_sc, db_sc, m_sc, l_sc, acc_sc, *, i_lo, i_hi, bq, lam_init):
    slopes2 = [2.0 ** (-8.0 * (h + 1) / N_HEADS) * LOG2E for h in range(N_HEADS)]
    cols = [slice(h * V_DIM, (h + 1) * V_DIM) for h in range(N_HEADS)]
    n_tiles = 2 * bq // LANES
    q_tiles = n_tiles // 2

    @pl.when(pl.program_id(0) == 0)
    def _():
        krow = lax.broadcasted_iota(jnp.int32, (bq, LANES), 0).astype(F32)
        kl, ql, rel = _in_block_bias(bq, bq, 1.0)
        visible = (kl // CHUNK) <= (ql // CHUNK)
        for h in range(N_HEADS):
            cb_sc[h] = slopes2[h] * krow
            db_sc[h] = jnp.where(visible, slopes2[h] * rel, -jnp.inf)

    for sel, q_ref in enumerate((q_lo_ref, q_hi_ref)):
        for h in range(N_HEADS):
            qq_sc[sel, h] = _two_map_queries(q_ref[0, :, cols[h]])
    m_sc[...] = jnp.full(m_sc.shape, -jnp.inf, F32)
    l_sc[...] = jnp.zeros(l_sc.shape, F32)
    acc_sc[...] = jnp.zeros(acc_sc.shape, F32)

    steps = [(0, j, j == i_lo) for j in range(i_lo + 1)] + [(1, j, j == i_hi) for j in range(i_hi + 1)]

    def scores(t):
        sel, j, _ = steps[t]
        return [lax.dot_general(k_ref[0, j * bq:(j + 1) * bq, cols[h]], qq_sc[sel, h], _NT,
                                preferred_element_type=F32) for h in range(N_HEADS)]

    def softmax(t, s_heads):
        sel, _, diag = steps[t]
        out = []
        for h, s in enumerate(s_heads):
            if diag:
                tiles = [s[:, LANES * n:LANES * (n + 1)]
                         + db_sc[h, :, LANES * (n % q_tiles):LANES * (n % q_tiles + 1)] for n in range(n_tiles)]
            else:
                tiles = [s[:, LANES * n:LANES * (n + 1)] + cb_sc[h] for n in range(n_tiles)]
            out.append(_softmax_update(tiles, slopes2[h] * bq, m_sc.at[sel, h], l_sc.at[sel, h]))
        return out

    def values(t, p_heads):
        sel, j, _ = steps[t]
        for h, (p, alpha) in enumerate(p_heads):
            _accumulate_values(v_ref[0, j * bq:(j + 1) * bq, cols[h]], p, alpha, acc_sc.at[sel, h])

    s_next = scores(0)
    p_prev = None
    for t in range(len(steps)):
        s_cur = s_next
        if t + 1 < len(steps):
            s_next = scores(t + 1)
        p_cur = softmax(t, s_cur)
        if p_prev is not None:
            values(t - 1, p_prev)
        p_prev = p_cur
    values(len(steps) - 1, p_prev)

    lam = _diff_lambda(lq_ref, lk_ref, lam_init)
    for sel in range(2):
        for h in range(N_HEADS):
            o = _diff_finish(acc_sc[sel, h], l_sc[sel, h], bq, lam, g_ref[...], lam_init)
            o_ref[0, sel, :, cols[h]] = o.astype(o_ref.dtype)


def _attn_prompt(qb, kb, vb, lq, lk, g, *, lam_init, bq=256):
    B, L, _ = qb.shape
    n = L // bq
    small = lambda shape: pl.BlockSpec(shape, lambda b: (0, 0))
    full = pl.BlockSpec((1, L, D_ATT), lambda b: (b, 0, 0))
    pairs = []
    for i_lo in range(n // 2):
        i_hi = n - 1 - i_lo
        pairs.append(pl.pallas_call(
            functools.partial(_attn_prompt_kernel, i_lo=i_lo, i_hi=i_hi, bq=bq, lam_init=lam_init),
            grid=(B,),
            in_specs=[pl.BlockSpec((1, bq, D_ATT), lambda b, i=i_lo: (b, i, 0)),
                      pl.BlockSpec((1, bq, D_ATT), lambda b, i=i_hi: (b, i, 0)),
                      full, full, small((2, HEAD_DIM)), small((2, HEAD_DIM)), small((1, V_DIM))],
            out_specs=pl.BlockSpec((1, 2, bq, D_ATT), lambda b: (b, 0, 0, 0)),
            out_shape=jax.ShapeDtypeStruct((B, 2, bq, D_ATT), BF16),
            scratch_shapes=[pltpu.VMEM((2, N_HEADS, 2 * bq, V_DIM), BF16),
                            pltpu.VMEM((N_HEADS, bq, LANES), F32), pltpu.VMEM((N_HEADS, bq, bq), F32),
                            pltpu.VMEM((2, N_HEADS, 1, 2 * bq), F32), pltpu.VMEM((2, N_HEADS, 1, 2 * bq), F32),
                            pltpu.VMEM((2, N_HEADS, V_DIM, 2 * bq), F32)],
            compiler_params=pltpu.CompilerParams(dimension_semantics=("arbitrary",),
                                                 vmem_limit_bytes=V7X_VMEM_LIMIT),
            name=f"attn_prompt_{i_lo}",
        )(qb, qb, kb, vb, lq, lk, g))
    blocks = [p[:, 0] for p in pairs] + [p[:, 1] for p in reversed(pairs)]
    return jnp.stack(blocks, axis=1).reshape(B, L, D_ATT)


def _attn_sample_kernel(q_ref, kc_ref, vc_ref, kn_ref, vn_ref, lq_ref, lk_ref, g_ref, o_ref,
                        qq_sc, m_sc, l_sc, acc_sc, *, nq, bk, lam_init):
    j = pl.program_id(1)

    @pl.when(j == 0)
    def _():
        for h in range(N_HEADS):
            qq_sc[h] = _two_map_queries(q_ref[0, :, h * V_DIM:(h + 1) * V_DIM])
        m_sc[...] = jnp.full(m_sc.shape, -jnp.inf, F32)
        l_sc[...] = jnp.zeros(l_sc.shape, F32)
        acc_sc[...] = jnp.zeros(acc_sc.shape, F32)

    slopes2 = [2.0 ** (-8.0 * (h + 1) / N_HEADS) * LOG2E for h in range(N_HEADS)]

    def block(k_blk_ref, v_blk_ref, n_keys, rel):
        head_rows = lambda ref, h: ref[0, pl.ds(h, n_keys, stride=N_HEADS), :].astype(BF16)
        scores = [lax.dot_general(head_rows(k_blk_ref, h), qq_sc[h], _NT, preferred_element_type=F32)
                  for h in range(N_HEADS)]
        probs = [_softmax_update([s + slopes2[h] * rel], slopes2[h] * bk, m_sc.at[h], l_sc.at[h])
                 for h, s in enumerate(scores)]
        for h, (p, alpha) in enumerate(probs):
            _accumulate_values(head_rows(v_blk_ref, h), p, alpha, acc_sc.at[h])

    block(kc_ref, vc_ref, bk, lax.broadcasted_iota(jnp.int32, (bk, 2 * nq), 0).astype(F32))

    @pl.when(j == pl.num_programs(1) - 1)
    def _():
        kl = lax.broadcasted_iota(jnp.int32, (nq, 2 * nq), 0)
        lane = lax.broadcasted_iota(jnp.int32, (nq, 2 * nq), 1)
        ql = jnp.where(lane >= nq, lane - nq, lane)
        rel = (ql - jnp.abs(ql - kl)).astype(F32)
        block(kn_ref, vn_ref, nq, rel)
        lam = _diff_lambda(lq_ref, lk_ref, lam_init)
        for h in range(N_HEADS):
            o = _diff_finish(acc_sc[h], l_sc[h], nq, lam, g_ref[...], lam_init)
            o_ref[0, :, h * V_DIM:(h + 1) * V_DIM] = o.astype(o_ref.dtype)


def _attn_sample(qb, k_cache, v_cache, k_new, v_new, lq, lk, g, *, lam_init, bk=1024):
    B, nq, _ = qb.shape
    past = k_cache.shape[1] // N_HEADS
    q_spec = pl.BlockSpec((1, nq, D_ATT), lambda b, j: (b, 0, 0))
    new_spec = pl.BlockSpec((1, nq * N_HEADS, V_DIM), lambda b, j: (b, 0, 0))
    cache_spec = pl.BlockSpec((1, bk * N_HEADS, V_DIM), lambda b, j: (b, j, 0))
    small = lambda shape: pl.BlockSpec(shape, lambda b, j: (0, 0))
    return pl.pallas_call(
        functools.partial(_attn_sample_kernel, nq=nq, bk=bk, lam_init=lam_init),
        grid=(B, past // bk),
        in_specs=[q_spec, cache_spec, cache_spec, new_spec, new_spec,
                  small((2, HEAD_DIM)), small((2, HEAD_DIM)), small((1, V_DIM))],
        out_specs=q_spec,
        out_shape=jax.ShapeDtypeStruct((B, nq, D_ATT), BF16),
        scratch_shapes=[pltpu.VMEM((N_HEADS, 2 * nq, V_DIM), BF16), pltpu.VMEM((N_HEADS, 1, 2 * nq), F32),
                        pltpu.VMEM((N_HEADS, 1, 2 * nq), F32), pltpu.VMEM((N_HEADS, V_DIM, 2 * nq), F32)],
        compiler_params=pltpu.CompilerParams(dimension_semantics=("arbitrary", "arbitrary"),
                                             vmem_limit_bytes=V7X_VMEM_LIMIT),
        name="attn_sample",
    )(qb, k_cache, v_cache, k_new, v_new, lq, lk, g)


def _mem_kv_kernel(m_ref, g_ref, wck_ref, wcv_ref, mk_ref, mv_ref, mkb_ref, mvb_ref):
    mn = _rmsnorm(m_ref[...], g_ref[...]).astype(BF16)
    k = _dot(mn, wck_ref[...])
    v = _dot(mn, wcv_ref[...])
    for s in range(mk_ref.shape[0]):
        rows = slice(s * N_MEM, (s + 1) * N_MEM)
        _store_heads(mk_ref.at[s], k[rows], N_MEM_HEADS, MEM_HEAD_DIM)
        _store_heads(mv_ref.at[s], v[rows], N_MEM_HEADS, MEM_HEAD_DIM)
    mkb_ref[...] = k.astype(BF16)
    mvb_ref[...] = v.astype(BF16)


def _mem_kv(mem2d, g, wck, wcv, *, tm=512):
    T = mem2d.shape[0]
    spt = tm // N_MEM
    tok = pl.BlockSpec((tm, D_MODEL), lambda i: (i, 0))
    heads = pl.BlockSpec((spt, N_MEM, N_MEM_HEADS, MEM_HEAD_DIM), lambda i: (i, 0, 0, 0))
    f = jax.ShapeDtypeStruct((T // N_MEM, N_MEM, N_MEM_HEADS, MEM_HEAD_DIM), F32)
    b = jax.ShapeDtypeStruct((T, D_MODEL), BF16)
    return pl.pallas_call(
        _mem_kv_kernel,
        grid=(T // tm,),
        in_specs=[tok, _const_spec((1, D_MODEL)), _const_spec(wck.shape), _const_spec(wcv.shape)],
        out_specs=(heads, heads, tok, tok),
        out_shape=(f, f, b, b),
        compiler_params=pltpu.CompilerParams(dimension_semantics=("arbitrary",),
                                             vmem_limit_bytes=V7X_VMEM_LIMIT),
        name="mem_kv",
    )(mem2d, g, wck, wcv)


def _tail_kernel(h_ref, ys_ref, ya_ref, mk_ref, mv_ref, wout_ref, gc_ref, wcq_ref, wco_ref,
                 g2_ref, wgu_ref, wd_ref, gf_ref, y_ref, *, nb, rpb, final_norm):
    h = h_ref[...] + _dot(ys_ref[...], wout_ref[:D_SSM, :]) + _dot(ya_ref[...], wout_ref[D_SSM:, :])

    qn = _rmsnorm(h, gc_ref[...]).astype(BF16)
    q = (_dot(qn, wcq_ref[...]) * (MEM_HEAD_DIM ** -0.5 * LOG2E)).astype(BF16)
    rows = []
    for lb in range(nb):
        heads = []
        for hh in range(N_MEM_HEADS):
            cols = slice(hh * MEM_HEAD_DIM, (hh + 1) * MEM_HEAD_DIM)
            qh = q[lb * rpb:(lb + 1) * rpb, cols]
            s = lax.dot_general(qh, mk_ref[lb, :, cols], _NT, preferred_element_type=F32)
            p = jnp.exp2(s - jnp.max(s, axis=1, keepdims=True))
            o = _dot(p.astype(BF16), mv_ref[lb, :, cols]) / jnp.sum(p, axis=1, keepdims=True)
            heads.append(o)
        rows.append(jnp.concatenate(heads, axis=1))
    o = rows[0] if nb == 1 else jnp.concatenate(rows, axis=0)
    h = h + _dot(o.astype(BF16), wco_ref[...])

    xn = _rmsnorm(h, g2_ref[...]).astype(BF16)
    x2 = h + 0.5 * _swiglu_half(xn, wgu_ref, wd_ref)
    y_ref[...] = _rmsnorm(x2, gf_ref[...]) if final_norm else x2


def _tail(h2d, ys2d, ya2d, mkb, mvb, wout, gc, wcq, wco, g2, wgu, wd, gf, *, seq, tm, final_norm):
    T = h2d.shape[0]
    if seq >= tm:
        nb, rpb = 1, tm
        nt = seq // tm
        mem_map = lambda i: (i // nt, 0, 0)
    else:
        nb, rpb = tm // seq, seq
        mem_map = lambda i: (i, 0, 0)
    tok = lambda w: pl.BlockSpec((tm, w), lambda i: (i, 0))
    mem_spec = pl.BlockSpec((nb, N_MEM, D_MODEL), mem_map)
    return pl.pallas_call(
        functools.partial(_tail_kernel, nb=nb, rpb=rpb, final_norm=final_norm),
        grid=(T // tm,),
        in_specs=[tok(D_MODEL), tok(D_SSM), tok(D_ATT), mem_spec, mem_spec,
                  _const_spec(wout.shape), _const_spec((1, D_MODEL)), _const_spec(wcq.shape),
                  _const_spec(wco.shape), _const_spec((1, D_MODEL)), _const_spec(wgu.shape),
                  _const_spec(wd.shape), _const_spec((1, D_MODEL))],
        out_specs=tok(D_MODEL),
        out_shape=jax.ShapeDtypeStruct((T, D_MODEL), F32),
        compiler_params=pltpu.CompilerParams(dimension_semantics=("arbitrary",),
                                             vmem_limit_bytes=V7X_VMEM_LIMIT),
        name="tail",
    )(h2d, ys2d, ya2d, mkb, mvb, wout, gc, wcq, wco, g2, wgu, wd, gf)


def kernel(x_prompt, x_sample, cache_attn_k, cache_attn_v, state_s5_re, state_s5_im, cache_mem_k, cache_mem_v, mem_prompt, g_ffn1, w_ffn1_gu, w_ffn1_d, g_mix, w_in, ssm_a_re, ssm_a_im, ssm_log_dt, ssm_b_re, ssm_b_im, ssm_c_re, ssm_c_im, ssm_d, w_glu, b_glu, lambda_q, lambda_k, g_subln, w_out, g_mem, g_cross, w_cq, w_ck, w_cv, w_co, g_ffn2, w_ffn2_gu, w_ffn2_d, g_final):
    depth = g_ffn1.shape[0]
    B, L, D = x_prompt.shape
    SB, SL, _ = x_sample.shape
    row = lambda a: a.reshape(1, -1)

    xp = x_prompt.reshape(B * L, D)
    xs = x_sample.reshape(SB * SL, D)
    outs = [[] for _ in range(10)]
    for l in range(depth):
        lam_init = 0.8 - 0.6 * math.exp(-0.3 * l)
        wgu1, wd1, win = w_ffn1_gu[l].astype(BF16), w_ffn1_d[l].astype(BF16), w_in[l].astype(BF16)
        wgu2, wd2 = w_ffn2_gu[l].astype(BF16), w_ffn2_d[l].astype(BF16)
        wout, wcq, wco = w_out[l].astype(BF16), w_cq[l].astype(BF16), w_co[l].astype(BF16)
        wck, wcv, wglu = w_ck[l].astype(BF16), w_cv[l].astype(BF16), w_glu[l].astype(BF16)

        abr, abi, bbr, bbi = _s5_prep(ssm_a_re[l], ssm_a_im[l], ssm_log_dt[l], ssm_b_re[l], ssm_b_im[l])
        ab = jnp.stack([abr.reshape(N_STATE), abi.reshape(N_STATE)])
        bbig = jnp.concatenate([_block_diag_in(bbr), _block_diag_in(bbi)], axis=2).astype(BF16)
        cbig = jnp.concatenate([_block_diag_out(ssm_c_re[l]), -_block_diag_out(ssm_c_im[l])], axis=1).astype(BF16)
        s5_w = (ab, bbig, cbig, row(ssm_d[l]), wglu, row(b_glu[l]))
        att_w = (lambda_q[l], lambda_k[l], row(g_subln[l]))
        tail_w = (wout, row(g_cross[l]), wcq, wco, row(g_ffn2[l]), wgu2, wd2, row(g_final))
        last = l == depth - 1

        hp, up, qp, kp, vp, kpb, vpb = _ffn_mix(xp, row(g_ffn1[l]), wgu1, wd1, row(g_mix[l]), win,
                                                 tm=512, bf16_kv=True)
        zeros = jnp.zeros((B, 2 * N_STATE), F32)
        yp_ssm, hlp = _s5(up.reshape(B, L, D_SSM), zeros, *s5_w, tc=CHUNK, slab=512)
        yp_att = _attn_prompt(qp.reshape(B, L, D_ATT), kpb.reshape(B, L, D_ATT), vpb.reshape(B, L, D_ATT),
                              *att_w, lam_init=lam_init)
        mkp, mvp, mkpb, mvpb = _mem_kv(mem_prompt.reshape(B * N_MEM, D), row(g_mem[l]), wck, wcv)
        xp = _tail(hp, yp_ssm.reshape(B * L, D_SSM), yp_att.reshape(B * L, D_ATT),
                   mkpb.reshape(B, N_MEM, D), mvpb.reshape(B, N_MEM, D), *tail_w,
                   seq=L, tm=512, final_norm=last)

        hs, us, qs, ks, vs = _ffn_mix(xs, row(g_ffn1[l]), wgu1, wd1, row(g_mix[l]), win,
                                      tm=512, bf16_kv=False)
        ks = ks.reshape(SB, SL, N_HEADS, V_DIM)
        vs = vs.reshape(SB, SL, N_HEADS, V_DIM)
        h0 = _state_to_cols(state_s5_re[l], state_s5_im[l])
        ys_ssm, hls = _s5(us.reshape(SB, SL, D_SSM), h0, *s5_w, tc=SL, slab=256)
        key_head_rows = lambda a: a.reshape(SB, -1, V_DIM)
        ys_att = _attn_sample(qs.reshape(SB, SL, D_ATT), key_head_rows(cache_attn_k[l]),
                              key_head_rows(cache_attn_v[l]), key_head_rows(ks), key_head_rows(vs),
                              *att_w, lam_init=lam_init)
        xs = _tail(hs, ys_ssm.reshape(SB * SL, D_SSM), ys_att.reshape(SB * SL, D_ATT),
                   cache_mem_k[l].reshape(SB, N_MEM, D).astype(BF16),
                   cache_mem_v[l].reshape(SB, N_MEM, D).astype(BF16), *tail_w,
                   seq=SL, tm=256, final_norm=last)

        rep, imp = _cols_to_state(hlp)
        res, ims = _cols_to_state(hls)
        layer = (kp.reshape(B, L, N_HEADS, V_DIM), vp.reshape(B, L, N_HEADS, V_DIM), rep, imp,
                 mkp, mvp,
                 ks, vs, res, ims)
        for o, a in zip(outs, layer):
            o.append(a)

    return (xp.reshape(B, L, D), xs.reshape(SB, SL, D)) + tuple(jnp.stack(o) for o in outs)
```

```python
import functools
import math

import jax
import jax.numpy as jnp
from jax import lax
from jax.experimental import pallas as pl
from jax.experimental.pallas import tpu as pltpu

F32 = jnp.float32
BF16 = jnp.bfloat16

D_MODEL = 1024
CHUNK = 64
D_SSM = 512
SSM_GROUP = 16
N_SSM_GROUPS = 32
SSM_STATE = 64
D_ATT = 512
N_HEADS = 4
HEAD_DIM = 64
V_DIM = 128
D_IN = D_SSM + 3 * D_ATT
N_MEM = 256
N_MEM_HEADS = 4
MEM_HEAD_DIM = 256
D_FF = 2816
EPS = 1e-6
LOG2E = math.log2(math.e)

N_STATE = N_SSM_GROUPS * SSM_STATE
SSM_BLOCKS = 2
BLK_CH = D_SSM // SSM_BLOCKS
BLK_ST = N_STATE // SSM_BLOCKS

LANES = 128
MXU_DIM = 256
FF_BOUNDS = tuple(range(0, D_FF, 3 * MXU_DIM)) + (D_FF,)
V7X_VMEM_LIMIT = 56 * 1024 * 1024

_NT = (((1,), (1,)), ((), ()))
_TN = (((0,), (0,)), ((), ()))


def _rmsnorm(x, g):
    return x * lax.rsqrt(jnp.mean(x * x, axis=-1, keepdims=True) + EPS) * g


def _dot(a, b):
    return jnp.dot(a, b, preferred_element_type=F32)


def _const_spec(shape):
    nd = len(shape)
    return pl.BlockSpec(shape, lambda *_: (0,) * nd, pipeline_mode=pl.Buffered(1))


def _swiglu_half(xn, wgu_ref, wd_ref):
    def gate_up(c):
        lo, hi = FF_BOUNDS[c], FF_BOUNDS[c + 1]
        return _dot(xn, wgu_ref[:, lo:hi]), _dot(xn, wgu_ref[:, D_FF + lo:D_FF + hi])

    n_chunks = len(FF_BOUNDS) - 1
    acc = None
    nxt = gate_up(0)
    for c in range(n_chunks):
        gate, up = nxt
        if c + 1 < n_chunks:
            nxt = gate_up(c + 1)
        act = (gate * jax.nn.sigmoid(gate) * up).astype(BF16)
        part = _dot(act, wd_ref[FF_BOUNDS[c]:FF_BOUNDS[c + 1], :])
        acc = part if acc is None else acc + part
    return acc


def _store_heads(ref, x, n_heads, width):
    for hh in range(n_heads):
        ref[:, hh, :] = x[:, hh * width:(hh + 1) * width]


def _ffn_mix_kernel(x_ref, g1_ref, wgu_ref, wd_ref, gm_ref, win_ref,
                    h_ref, u_ref, qb_ref, k_ref, v_ref, *kvb_refs):
    x = x_ref[...]
    xn = _rmsnorm(x, g1_ref[...]).astype(BF16)
    h = x + 0.5 * _swiglu_half(xn, wgu_ref, wd_ref)
    h_ref[...] = h
    hn = _rmsnorm(h, gm_ref[...]).astype(BF16)
    z = _dot(hn, win_ref[...])
    u_ref[...] = z[:, :D_SSM]
    q = z[:, D_SSM:D_SSM + D_ATT]
    k = z[:, D_SSM + D_ATT:D_SSM + 2 * D_ATT]
    v = z[:, D_SSM + 2 * D_ATT:]
    qb_ref[...] = (q * (HEAD_DIM ** -0.5 * LOG2E)).astype(BF16)
    _store_heads(k_ref, k, N_HEADS, V_DIM)
    _store_heads(v_ref, v, N_HEADS, V_DIM)
    if kvb_refs:
        kb_ref, vb_ref = kvb_refs
        kb_ref[...] = k.astype(BF16)
        vb_ref[...] = v.astype(BF16)


def _ffn_mix(x2d, g1, wgu, wd, gm, win, *, tm, bf16_kv):
    T = x2d.shape[0]
    tok = lambda w: pl.BlockSpec((tm, w), lambda i: (i, 0))
    heads = pl.BlockSpec((tm, N_HEADS, V_DIM), lambda i: (i, 0, 0))
    out_shape = [
        jax.ShapeDtypeStruct((T, D_MODEL), F32),
        jax.ShapeDtypeStruct((T, D_SSM), F32),
        jax.ShapeDtypeStruct((T, D_ATT), BF16),
        jax.ShapeDtypeStruct((T, N_HEADS, V_DIM), F32),
        jax.ShapeDtypeStruct((T, N_HEADS, V_DIM), F32),
    ]
    out_specs = [tok(D_MODEL), tok(D_SSM), tok(D_ATT), heads, heads]
    if bf16_kv:
        out_shape += [jax.ShapeDtypeStruct((T, D_ATT), BF16)] * 2
        out_specs += [tok(D_ATT)] * 2
    return pl.pallas_call(
        _ffn_mix_kernel,
        grid=(T // tm,),
        in_specs=[tok(D_MODEL), _const_spec((1, D_MODEL)), _const_spec(wgu.shape), _const_spec(wd.shape),
                  _const_spec((1, D_MODEL)), _const_spec(win.shape)],
        out_specs=tuple(out_specs),
        out_shape=tuple(out_shape),
        compiler_params=pltpu.CompilerParams(dimension_semantics=("arbitrary",),
                                             vmem_limit_bytes=V7X_VMEM_LIMIT),
        name="ffn_mix",
    )(x2d, g1, wgu, wd, gm, win)


def _s5_prep_kernel(are_ref, aim_ref, ldt_ref, bre_ref, bim_ref, abr_ref, abi_ref, bbr_ref, bbi_ref):
    lre = are_ref[...]
    lim = aim_ref[...]
    dt = jnp.exp(ldt_ref[...])
    mag = jnp.exp(lre * dt)
    ar = mag * jnp.cos(lim * dt)
    ai = mag * jnp.sin(lim * dt)
    abr_ref[...] = ar
    abi_ref[...] = ai
    den = lre * lre + lim * lim
    xr = ar - 1.0
    cr = (xr * lre + ai * lim) / den
    ci = (ai * lre - xr * lim) / den
    br = bre_ref[...]
    bi = bim_ref[...]
    bbr_ref[...] = cr * br - ci * bi
    bbi_ref[...] = cr * bi + ci * br


def _s5_prep(a_re, a_im, log_dt, b_re, b_im):
    G, P, H = N_SSM_GROUPS, SSM_STATE, SSM_GROUP
    per_row = lambda a: jnp.repeat(a, H, axis=0)
    b_re_t = jnp.swapaxes(b_re, 1, 2).reshape(G * H, P)
    b_im_t = jnp.swapaxes(b_im, 1, 2).reshape(G * H, P)
    o = jax.ShapeDtypeStruct((G * H, P), F32)
    abr, abi, bbr, bbi = pl.pallas_call(
        _s5_prep_kernel,
        out_shape=(o, o, o, o),
        name="s5_prep",
    )(per_row(a_re), per_row(a_im), per_row(log_dt.reshape(G, 1)), b_re_t, b_im_t)
    return (abr.reshape(G, H, P)[:, 0], abi.reshape(G, H, P)[:, 0], bbr.reshape(G, H, P), bbi.reshape(G, H, P))


def _block_diag_in(w_ghp):
    gpb = N_SSM_GROUPS // SSM_BLOCKS
    w = w_ghp.reshape(SSM_BLOCKS, gpb, SSM_GROUP, SSM_STATE)
    eye = jnp.eye(gpb, dtype=w.dtype)
    return jnp.einsum('jghp,gk->jghkp', w, eye).reshape(SSM_BLOCKS, BLK_CH, BLK_ST)


def _block_diag_out(w_ghp):
    gpb = N_SSM_GROUPS // SSM_BLOCKS
    w = w_ghp.reshape(SSM_BLOCKS, gpb, SSM_GROUP, SSM_STATE)
    eye = jnp.eye(gpb, dtype=w.dtype)
    return jnp.einsum('jghp,gk->jgpkh', w, eye).reshape(SSM_BLOCKS, BLK_ST, BLK_CH)


def _gelu_tanh(x):
    return 0.5 * x * (1.0 + jnp.tanh(math.sqrt(2.0 / math.pi) * (x + 0.044715 * (x * x * x))))


def _s5_kernel(u_ref, h0_ref, ab_ref, bbig_ref, cbig_ref, d_ref, wglu_ref, bglu_ref,
               y_ref, hl_ref, hst, abr, abi, io_sc, bu_sc, *, nb, tc, slab):
    c = pl.program_id(0)
    n_lt = D_SSM // LANES

    @pl.when(c == 0)
    def _():
        hst[...] = h0_ref[...]
        abr[...] = jnp.broadcast_to(ab_ref[0:1, :], (nb, N_STATE))
        abi[...] = jnp.broadcast_to(ab_ref[1:2, :], (nb, N_STATE))

    for b in range(nb):
        for lt in range(n_lt):
            io_sc[lt, pl.ds(b, tc, stride=nb), :] = u_ref[b, :, lt * LANES:(lt + 1) * LANES]
    u = jnp.concatenate([io_sc[lt] for lt in range(n_lt)], axis=1)
    ub = u.astype(BF16)
    for j in range(SSM_BLOCKS):
        bu_sc[j] = _dot(ub[:, BLK_CH * j:BLK_CH * (j + 1)], bbig_ref[j])
    ys = []
    for j in range(SSM_BLOCKS):
        for s in range(BLK_ST // slab):
            cr = slice(s * slab, (s + 1) * slab)
            ci = slice(BLK_ST + s * slab, BLK_ST + (s + 1) * slab)
            ar = abr[:, BLK_ST * j + s * slab:BLK_ST * j + (s + 1) * slab]
            ai = abi[:, BLK_ST * j + s * slab:BLK_ST * j + (s + 1) * slab]
            sr = slice(2 * BLK_ST * j + s * slab, 2 * BLK_ST * j + (s + 1) * slab)
            si = slice(2 * BLK_ST * j + BLK_ST + s * slab, 2 * BLK_ST * j + BLK_ST + (s + 1) * slab)
            hr, hi = hst[:, sr], hst[:, si]
            for t in range(tc):
                rows = slice(t * nb, (t + 1) * nb)
                hr, hi = (ar * hr - ai * hi + bu_sc[j, rows, cr], ar * hi + ai * hr + bu_sc[j, rows, ci])
                bu_sc[j, rows, cr] = hr
                bu_sc[j, rows, ci] = hi
            hst[:, sr] = hr
            hst[:, si] = hi
        ys.append(_dot(bu_sc[j].astype(BF16), cbig_ref[j]))
    y = jnp.concatenate(ys, axis=1) + d_ref[...] * u
    y = _gelu_tanh(y)
    gate = _dot(y.astype(BF16), wglu_ref[...]) + bglu_ref[...]
    out = y * jax.nn.sigmoid(gate)
    for lt in range(n_lt):
        io_sc[lt] = out[:, lt * LANES:(lt + 1) * LANES]
    for b in range(nb):
        for lt in range(n_lt):
            y_ref[b, :, lt * LANES:(lt + 1) * LANES] = io_sc[lt, pl.ds(b, tc, stride=nb), :].astype(BF16)

    @pl.when(c == pl.num_programs(0) - 1)
    def _():
        hl_ref[...] = hst[...]


def _s5(u, h0, ab, bbig, cbig, d, wglu, bglu, *, tc, slab):
    nb, seq, _ = u.shape
    rows = tc * nb
    io_spec = pl.BlockSpec((nb, tc, D_SSM), lambda c: (0, c, 0))
    return pl.pallas_call(
        functools.partial(_s5_kernel, nb=nb, tc=tc, slab=slab),
        grid=(seq // tc,),
        in_specs=[io_spec,
                  _const_spec((nb, 2 * N_STATE)), _const_spec((2, N_STATE)),
                  _const_spec(bbig.shape), _const_spec(cbig.shape),
                  _const_spec((1, D_SSM)), _const_spec((D_SSM, D_SSM)), _const_spec((1, D_SSM))],
        out_specs=(io_spec, pl.BlockSpec((nb, 2 * N_STATE), lambda c: (0, 0))),
        out_shape=(jax.ShapeDtypeStruct((nb, seq, D_SSM), BF16),
                   jax.ShapeDtypeStruct((nb, 2 * N_STATE), F32)),
        scratch_shapes=[pltpu.VMEM((nb, 2 * N_STATE), F32), pltpu.VMEM((nb, N_STATE), F32),
                        pltpu.VMEM((nb, N_STATE), F32), pltpu.VMEM((D_SSM // LANES, rows, LANES), F32),
                        pltpu.VMEM((SSM_BLOCKS, rows, 2 * BLK_ST), F32)],
        compiler_params=pltpu.CompilerParams(dimension_semantics=("arbitrary",),
                                             vmem_limit_bytes=V7X_VMEM_LIMIT),
        name="s5_scan",
    )(u, h0, ab, bbig, cbig, d, wglu, bglu)


def _state_to_cols(s_re, s_im):
    nb = s_re.shape[0]
    r = s_re.reshape(nb, SSM_BLOCKS, 1, BLK_ST)
    i = s_im.reshape(nb, SSM_BLOCKS, 1, BLK_ST)
    return jnp.concatenate([r, i], axis=2).reshape(nb, 2 * N_STATE)


def _cols_to_state(h):
    nb = h.shape[0]
    h4 = h.reshape(nb, SSM_BLOCKS, 2, BLK_ST)
    return (h4[:, :, 0].reshape(nb, N_SSM_GROUPS, SSM_STATE), h4[:, :, 1].reshape(nb, N_SSM_GROUPS, SSM_STATE))


def _two_map_queries(q):
    lane = lax.broadcasted_iota(jnp.int32, q.shape, 1)
    lo = lane < HEAD_DIM
    zero = jnp.zeros_like(q)
    return jnp.concatenate([jnp.where(lo, q, zero), jnp.where(lo, zero, q)], axis=0)


def _diff_lambda(lq_ref, lk_ref, lam_init):
    p = lq_ref[...] * lk_ref[...]
    s = jnp.sum(p, axis=1, keepdims=True)
    e = jnp.exp(s)
    return e[0:1, :] - e[1:2, :] + lam_init


def _softmax_update(s_tiles, shift, m_sc, l_sc):
    m_in = m_sc[...] - shift
    mx = jnp.concatenate([jnp.max(t, axis=0, keepdims=True) for t in s_tiles], axis=1)
    m_new = jnp.maximum(m_in, mx)
    alpha = jnp.exp2(m_in - m_new)
    p_tiles = [jnp.exp2(t - m_new[:, LANES * n:LANES * (n + 1)]) for n, t in enumerate(s_tiles)]
    l_sc[...] = alpha * l_sc[...] + jnp.concatenate([jnp.sum(p, axis=0, keepdims=True) for p in p_tiles], axis=1)
    m_sc[...] = m_new
    return jnp.concatenate(p_tiles, axis=1).astype(BF16), alpha


def _accumulate_values(vb, p, alpha, acc_sc):
    acc_sc[...] = alpha * acc_sc[...] + lax.dot_general(vb, p, _TN, preferred_element_type=F32)


def _softmax_block(s_tiles, vb, shift, m_sc, l_sc, acc_sc):
    p, alpha = _softmax_update(s_tiles, shift, m_sc, l_sc)
    _accumulate_values(vb, p, alpha, acc_sc)


def _emit_pipelined(n_steps, scores, softmax, values):
    s_next = scores(0)
    p_prev = None
    for t in range(n_steps):
        s_cur = s_next
        if t + 1 < n_steps:
            s_next = scores(t + 1)
        p_cur = softmax(t, s_cur)
        if p_prev is not None:
            values(t - 1, p_prev)
        p_prev = p_cur
    values(n_steps - 1, p_prev)


def _diff_finish(acc_t, l, n, lam, g, lam_init):
    o = (acc_t * (1.0 / l)).T
    return _rmsnorm(o[:n] - lam * o[n:], g) * (1.0 - lam_init)


def _in_block_bias(n_keys, n_queries, slope2):
    kl = lax.broadcasted_iota(jnp.int32, (n_keys, n_queries), 0)
    ql = lax.broadcasted_iota(jnp.int32, (n_keys, n_queries), 1)
    return kl, ql, slope2 * (ql - jnp.abs(ql - kl)).astype(F32)


def _attn_prompt_kernel(q_lo_ref, q_hi_ref, k_ref, v_ref, lq_ref, lk_ref, g_ref, o_ref,
                        qq_sc, cb_sc, db_sc, m_sc, l_sc, acc_sc, *, i_lo, i_hi, bq, lam_init):
    slopes2 = [2.0 ** (-8.0 * (h + 1) / N_HEADS) * LOG2E for h in range(N_HEADS)]
    cols = [slice(h * V_DIM, (h + 1) * V_DIM) for h in range(N_HEADS)]
    n_tiles = 2 * bq // LANES
    q_tiles = n_tiles // 2

    @pl.when(pl.program_id(0) == 0)
    def _():
        krow = lax.broadcasted_iota(jnp.int32, (bq, LANES), 0).astype(F32)
        kl, ql, rel = _in_block_bias(bq, bq, 1.0)
        visible = (kl // CHUNK) <= (ql // CHUNK)
        for h in range(N_HEADS):
            cb_sc[h] = slopes2[h] * krow
            db_sc[h] = jnp.where(visible, slopes2[h] * rel, -jnp.inf)

    for sel, q_ref in enumerate((q_lo_ref, q_hi_ref)):
        for h in range(N_HEADS):
            qq_sc[sel, h] = _two_map_queries(q_ref[0, :, cols[h]])
    m_sc[...] = jnp.full(m_sc.shape, -jnp.inf, F32)
    l_sc[...] = jnp.zeros(l_sc.shape, F32)
    acc_sc[...] = jnp.zeros(acc_sc.shape, F32)

    steps = [(0, j, j == i_lo) for j in range(i_lo + 1)] + [(1, j, j == i_hi) for j in range(i_hi + 1)]

    def scores(t):
        sel, j, _ = steps[t]
        return [lax.dot_general(k_ref[0, j * bq:(j + 1) * bq, cols[h]], qq_sc[sel, h], _NT,
                                preferred_element_type=F32) for h in range(N_HEADS)]

    def softmax(t, s_heads):
        sel, _, diag = steps[t]
        out = []
        for h, s in enumerate(s_heads):
            if diag:
                tiles = [s[:, LANES * n:LANES * (n + 1)]
                         + db_sc[h, :, LANES * (n % q_tiles):LANES * (n % q_tiles + 1)] for n in range(n_tiles)]
            else:
                tiles = [s[:, LANES * n:LANES * (n + 1)] + cb_sc[h] for n in range(n_tiles)]
            out.append(_softmax_update(tiles, slopes2[h] * bq, m_sc.at[sel, h], l_sc.at[sel, h]))
        return out

    def values(t, p_heads):
        sel, j, _ = steps[t]
        for h, (p, alpha) in enumerate(p_heads):
            _accumulate_values(v_ref[0, j * bq:(j + 1) * bq, cols[h]], p, alpha, acc_sc.at[sel, h])

    _emit_pipelined(len(steps), scores, softmax, values)
    lam = _diff_lambda(lq_ref, lk_ref, lam_init)
    for sel in range(2):
        for h in range(N_HEADS):
            o = _diff_finish(acc_sc[sel, h], l_sc[sel, h], bq, lam, g_ref[...], lam_init)
            o_ref[0, sel, :, cols[h]] = o.astype(o_ref.dtype)


def _attn_prompt(qb, kb, vb, lq, lk, g, *, lam_init, bq=256):
    B, L, _ = qb.shape
    n = L // bq
    small = lambda shape: pl.BlockSpec(shape, lambda b: (0, 0))
    full = pl.BlockSpec((1, L, D_ATT), lambda b: (b, 0, 0))
    pairs = []
    for i_lo in range(n // 2):
        i_hi = n - 1 - i_lo
        pairs.append(pl.pallas_call(
            functools.partial(_attn_prompt_kernel, i_lo=i_lo, i_hi=i_hi, bq=bq, lam_init=lam_init),
            grid=(B,),
            in_specs=[pl.BlockSpec((1, bq, D_ATT), lambda b, i=i_lo: (b, i, 0)),
                      pl.BlockSpec((1, bq, D_ATT), lambda b, i=i_hi: (b, i, 0)),
                      full, full, small((2, HEAD_DIM)), small((2, HEAD_DIM)), small((1, V_DIM))],
            out_specs=pl.BlockSpec((1, 2, bq, D_ATT), lambda b: (b, 0, 0, 0)),
            out_shape=jax.ShapeDtypeStruct((B, 2, bq, D_ATT), BF16),
            scratch_shapes=[pltpu.VMEM((2, N_HEADS, 2 * bq, V_DIM), BF16),
                            pltpu.VMEM((N_HEADS, bq, LANES), F32), pltpu.VMEM((N_HEADS, bq, bq), F32),
                            pltpu.VMEM((2, N_HEADS, 1, 2 * bq), F32), pltpu.VMEM((2, N_HEADS, 1, 2 * bq), F32),
                            pltpu.VMEM((2, N_HEADS, V_DIM, 2 * bq), F32)],
            compiler_params=pltpu.CompilerParams(dimension_semantics=("arbitrary",),
                                                 vmem_limit_bytes=V7X_VMEM_LIMIT),
            name=f"attn_prompt_{i_lo}",
        )(qb, qb, kb, vb, lq, lk, g))
    blocks = [p[:, 0] for p in pairs] + [p[:, 1] for p in reversed(pairs)]
    return jnp.stack(blocks, axis=1).reshape(B, L, D_ATT)


def _attn_sample_kernel(q_ref, kc_ref, vc_ref, kn_ref, vn_ref, lq_ref, lk_ref, g_ref, o_ref,
                        qq_sc, m_sc, l_sc, acc_sc, *, nq, bk, lam_init):
    slopes2 = [2.0 ** (-8.0 * (h + 1) / N_HEADS) * LOG2E for h in range(N_HEADS)]
    n_cache = kc_ref.shape[1] // (bk * N_HEADS)
    for h in range(N_HEADS):
        qq_sc[h] = _two_map_queries(q_ref[0, :, h * V_DIM:(h + 1) * V_DIM])
    m_sc[...] = jnp.full(m_sc.shape, -jnp.inf, F32)
    l_sc[...] = jnp.zeros(l_sc.shape, F32)
    acc_sc[...] = jnp.zeros(acc_sc.shape, F32)

    krow = lax.broadcasted_iota(jnp.int32, (bk, 2 * nq), 0).astype(F32)
    kl = lax.broadcasted_iota(jnp.int32, (nq, 2 * nq), 0)
    lane = lax.broadcasted_iota(jnp.int32, (nq, 2 * nq), 1)
    ql = jnp.where(lane >= nq, lane - nq, lane)
    rel_new = (ql - jnp.abs(ql - kl)).astype(F32)

    def head_rows(t, h, cache_ref, new_ref):
        if t < n_cache:
            return cache_ref[0, pl.ds(t * bk * N_HEADS + h, bk, stride=N_HEADS), :].astype(BF16)
        return new_ref[0, pl.ds(h, nq, stride=N_HEADS), :].astype(BF16)

    def scores(t):
        return [lax.dot_general(head_rows(t, h, kc_ref, kn_ref), qq_sc[h], _NT, preferred_element_type=F32)
                for h in range(N_HEADS)]

    def softmax(t, s_heads):
        rel = krow if t < n_cache else rel_new
        return [_softmax_update([s + slopes2[h] * rel], slopes2[h] * bk, m_sc.at[h], l_sc.at[h])
                for h, s in enumerate(s_heads)]

    def values(t, p_heads):
        for h, (p, alpha) in enumerate(p_heads):
            _accumulate_values(head_rows(t, h, vc_ref, vn_ref), p, alpha, acc_sc.at[h])

    _emit_pipelined(n_cache + 1, scores, softmax, values)
    lam = _diff_lambda(lq_ref, lk_ref, lam_init)
    for h in range(N_HEADS):
        o = _diff_finish(acc_sc[h], l_sc[h], nq, lam, g_ref[...], lam_init)
        o_ref[0, :, h * V_DIM:(h + 1) * V_DIM] = o.astype(o_ref.dtype)


def _attn_sample(qb, k_cache, v_cache, k_new, v_new, lq, lk, g, *, lam_init, bk=1024):
    B, nq, _ = qb.shape
    q_spec = pl.BlockSpec((1, nq, D_ATT), lambda b: (b, 0, 0))
    new_spec = pl.BlockSpec((1, nq * N_HEADS, V_DIM), lambda b: (b, 0, 0))
    cache_spec = pl.BlockSpec((1, k_cache.shape[1], V_DIM), lambda b: (b, 0, 0))
    small = lambda shape: pl.BlockSpec(shape, lambda b: (0, 0))
    return pl.pallas_call(
        functools.partial(_attn_sample_kernel, nq=nq, bk=bk, lam_init=lam_init),
        grid=(B,),
        in_specs=[q_spec, cache_spec, cache_spec, new_spec, new_spec,
                  small((2, HEAD_DIM)), small((2, HEAD_DIM)), small((1, V_DIM))],
        out_specs=q_spec,
        out_shape=jax.ShapeDtypeStruct((B, nq, D_ATT), BF16),
        scratch_shapes=[pltpu.VMEM((N_HEADS, 2 * nq, V_DIM), BF16), pltpu.VMEM((N_HEADS, 1, 2 * nq), F32),
                        pltpu.VMEM((N_HEADS, 1, 2 * nq), F32), pltpu.VMEM((N_HEADS, V_DIM, 2 * nq), F32)],
        compiler_params=pltpu.CompilerParams(dimension_semantics=("arbitrary",),
                                             vmem_limit_bytes=V7X_VMEM_LIMIT),
        name="attn_sample",
    )(qb, k_cache, v_cache, k_new, v_new, lq, lk, g)


def _mem_kv_kernel(m_ref, g_ref, wck_ref, wcv_ref, mk_ref, mv_ref, mkb_ref, mvb_ref):
    mn = _rmsnorm(m_ref[...], g_ref[...]).astype(BF16)
    k = _dot(mn, wck_ref[...])
    v = _dot(mn, wcv_ref[...])
    for s in range(mk_ref.shape[0]):
        rows = slice(s * N_MEM, (s + 1) * N_MEM)
        _store_heads(mk_ref.at[s], k[rows], N_MEM_HEADS, MEM_HEAD_DIM)
        _store_heads(mv_ref.at[s], v[rows], N_MEM_HEADS, MEM_HEAD_DIM)
        for hh in range(N_MEM_HEADS):
            cols = slice(hh * MEM_HEAD_DIM, (hh + 1) * MEM_HEAD_DIM)
            mkb_ref[s, hh] = k[rows, cols].astype(BF16)
            mvb_ref[s, hh] = v[rows, cols].astype(BF16)


def _mem_kv(mem2d, g, wck, wcv, *, tm=512):
    T = mem2d.shape[0]
    spt = tm // N_MEM
    tok = pl.BlockSpec((tm, D_MODEL), lambda i: (i, 0))
    heads = pl.BlockSpec((spt, N_MEM, N_MEM_HEADS, MEM_HEAD_DIM), lambda i: (i, 0, 0, 0))
    head_major = pl.BlockSpec((spt, N_MEM_HEADS, N_MEM, MEM_HEAD_DIM), lambda i: (i, 0, 0, 0))
    f = jax.ShapeDtypeStruct((T // N_MEM, N_MEM, N_MEM_HEADS, MEM_HEAD_DIM), F32)
    b = jax.ShapeDtypeStruct((T // N_MEM, N_MEM_HEADS, N_MEM, MEM_HEAD_DIM), BF16)
    return pl.pallas_call(
        _mem_kv_kernel,
        grid=(T // tm,),
        in_specs=[tok, _const_spec((1, D_MODEL)), _const_spec(wck.shape), _const_spec(wcv.shape)],
        out_specs=(heads, heads, head_major, head_major),
        out_shape=(f, f, b, b),
        compiler_params=pltpu.CompilerParams(dimension_semantics=("arbitrary",),
                                             vmem_limit_bytes=V7X_VMEM_LIMIT),
        name="mem_kv",
    )(mem2d, g, wck, wcv)


def _tail_kernel(h_ref, ys_ref, ya_ref, mk_ref, mv_ref, wout_ref, gc_ref, wcq_ref, wco_ref,
                 g2_ref, wgu_ref, wd_ref, gf_ref, y_ref, *, nb, rpb, final_norm):
    h = h_ref[...] + _dot(ys_ref[...], wout_ref[:D_SSM, :]) + _dot(ya_ref[...], wout_ref[D_SSM:, :])

    qn = _rmsnorm(h, gc_ref[...]).astype(BF16)
    q = (_dot(qn, wcq_ref[...]) * (MEM_HEAD_DIM ** -0.5 * LOG2E)).astype(BF16)
    rows = []
    for lb in range(nb):
        heads = []
        for hh in range(N_MEM_HEADS):
            cols = slice(hh * MEM_HEAD_DIM, (hh + 1) * MEM_HEAD_DIM)
            qh = q[lb * rpb:(lb + 1) * rpb, cols]
            s = lax.dot_general(qh, mk_ref[lb, hh], _NT, preferred_element_type=F32)
            p = jnp.exp2(s - jnp.max(s, axis=1, keepdims=True))
            o = _dot(p.astype(BF16), mv_ref[lb, hh]) / jnp.sum(p, axis=1, keepdims=True)
            heads.append(o)
        rows.append(jnp.concatenate(heads, axis=1))
    o = rows[0] if nb == 1 else jnp.concatenate(rows, axis=0)
    h = h + _dot(o.astype(BF16), wco_ref[...])

    xn = _rmsnorm(h, g2_ref[...]).astype(BF16)
    x2 = h + 0.5 * _swiglu_half(xn, wgu_ref, wd_ref)
    y_ref[...] = _rmsnorm(x2, gf_ref[...]) if final_norm else x2


def _tail(h2d, ys2d, ya2d, mkb, mvb, wout, gc, wcq, wco, g2, wgu, wd, gf, *, seq, tm, final_norm):
    T = h2d.shape[0]
    if seq >= tm:
        nb, rpb = 1, tm
        nt = seq // tm
        mem_map = lambda i: (i // nt, 0, 0, 0)
    else:
        nb, rpb = tm // seq, seq
        mem_map = lambda i: (i, 0, 0, 0)
    tok = lambda w: pl.BlockSpec((tm, w), lambda i: (i, 0))
    mem_spec = pl.BlockSpec((nb, N_MEM_HEADS, N_MEM, MEM_HEAD_DIM), mem_map)
    return pl.pallas_call(
        functools.partial(_tail_kernel, nb=nb, rpb=rpb, final_norm=final_norm),
        grid=(T // tm,),
        in_specs=[tok(D_MODEL), tok(D_SSM), tok(D_ATT), mem_spec, mem_spec,
                  _const_spec(wout.shape), _const_spec((1, D_MODEL)), _const_spec(wcq.shape),
                  _const_spec(wco.shape), _const_spec((1, D_MODEL)), _const_spec(wgu.shape),
                  _const_spec(wd.shape), _const_spec((1, D_MODEL))],
        out_specs=tok(D_MODEL),
        out_shape=jax.ShapeDtypeStruct((T, D_MODEL), F32),
        compiler_params=pltpu.CompilerParams(dimension_semantics=("arbitrary",),
                                             vmem_limit_bytes=V7X_VMEM_LIMIT),
        name="tail",
    )(h2d, ys2d, ya2d, mkb, mvb, wout, gc, wcq, wco, g2, wgu, wd, gf)


def kernel(x_prompt, x_sample, cache_attn_k, cache_attn_v, state_s5_re, state_s5_im, cache_mem_k, cache_mem_v, mem_prompt, g_ffn1, w_ffn1_gu, w_ffn1_d, g_mix, w_in, ssm_a_re, ssm_a_im, ssm_log_dt, ssm_b_re, ssm_b_im, ssm_c_re, ssm_c_im, ssm_d, w_glu, b_glu, lambda_q, lambda_k, g_subln, w_out, g_mem, g_cross, w_cq, w_ck, w_cv, w_co, g_ffn2, w_ffn2_gu, w_ffn2_d, g_final):
    depth = g_ffn1.shape[0]
    B, L, D = x_prompt.shape
    SB, SL, _ = x_sample.shape
    row = lambda a: a.reshape(1, -1)

    xp = x_prompt.reshape(B * L, D)
    xs = x_sample.reshape(SB * SL, D)
    outs = [[] for _ in range(10)]
    for l in range(depth):
        lam_init = 0.8 - 0.6 * math.exp(-0.3 * l)
        wgu1, wd1, win = w_ffn1_gu[l].astype(BF16), w_ffn1_d[l].astype(BF16), w_in[l].astype(BF16)
        wgu2, wd2 = w_ffn2_gu[l].astype(BF16), w_ffn2_d[l].astype(BF16)
        wout, wcq, wco = w_out[l].astype(BF16), w_cq[l].astype(BF16), w_co[l].astype(BF16)
        wck, wcv, wglu = w_ck[l].astype(BF16), w_cv[l].astype(BF16), w_glu[l].astype(BF16)

        abr, abi, bbr, bbi = _s5_prep(ssm_a_re[l], ssm_a_im[l], ssm_log_dt[l], ssm_b_re[l], ssm_b_im[l])
        ab = jnp.stack([abr.reshape(N_STATE), abi.reshape(N_STATE)])
        bbig = jnp.concatenate([_block_diag_in(bbr), _block_diag_in(bbi)], axis=2).astype(BF16)
        cbig = jnp.concatenate([_block_diag_out(ssm_c_re[l]), -_block_diag_out(ssm_c_im[l])], axis=1).astype(BF16)
        s5_w = (ab, bbig, cbig, row(ssm_d[l]), wglu, row(b_glu[l]))
        att_w = (lambda_q[l], lambda_k[l], row(g_subln[l]))
        tail_w = (wout, row(g_cross[l]), wcq, wco, row(g_ffn2[l]), wgu2, wd2, row(g_final))
        last = l == depth - 1

        hp, up, qp, kp, vp, kpb, vpb = _ffn_mix(xp, row(g_ffn1[l]), wgu1, wd1, row(g_mix[l]), win,
                                                 tm=512, bf16_kv=True)
        zeros = jnp.zeros((B, 2 * N_STATE), F32)
        yp_ssm, hlp = _s5(up.reshape(B, L, D_SSM), zeros, *s5_w, tc=CHUNK, slab=512)
        yp_att = _attn_prompt(qp.reshape(B, L, D_ATT), kpb.reshape(B, L, D_ATT), vpb.reshape(B, L, D_ATT),
                              *att_w, lam_init=lam_init)
        mkp, mvp, mkpb, mvpb = _mem_kv(mem_prompt.reshape(B * N_MEM, D), row(g_mem[l]), wck, wcv)
        xp = _tail(hp, yp_ssm.reshape(B * L, D_SSM), yp_att.reshape(B * L, D_ATT),
                   mkpb, mvpb, *tail_w,
                   seq=L, tm=512, final_norm=last)

        hs, us, qs, ks, vs = _ffn_mix(xs, row(g_ffn1[l]), wgu1, wd1, row(g_mix[l]), win,
                                      tm=512, bf16_kv=False)
        ks = ks.reshape(SB, SL, N_HEADS, V_DIM)
        vs = vs.reshape(SB, SL, N_HEADS, V_DIM)
        h0 = _state_to_cols(state_s5_re[l], state_s5_im[l])
        ys_ssm, hls = _s5(us.reshape(SB, SL, D_SSM), h0, *s5_w, tc=SL, slab=256)
        key_head_rows = lambda a: a.reshape(SB, -1, V_DIM)
        ys_att = _attn_sample(qs.reshape(SB, SL, D_ATT), key_head_rows(cache_attn_k[l]),
                              key_head_rows(cache_attn_v[l]), key_head_rows(ks), key_head_rows(vs),
                              *att_w, lam_init=lam_init)
        xs = _tail(hs, ys_ssm.reshape(SB * SL, D_SSM), ys_att.reshape(SB * SL, D_ATT),
                   jnp.swapaxes(cache_mem_k[l], 1, 2).astype(BF16),
                   jnp.swapaxes(cache_mem_v[l], 1, 2).astype(BF16), *tail_w,
                   seq=SL, tm=256, final_norm=last)

        rep, imp = _cols_to_state(hlp)
        res, ims = _cols_to_state(hls)
        layer = (kp.reshape(B, L, N_HEADS, V_DIM), vp.reshape(B, L, N_HEADS, V_DIM), rep, imp,
                 mkp, mvp,
                 ks, vs, res, ims)
        for o, a in zip(outs, layer):
            o.append(a)

    return (xp.reshape(B, L, D), xs.reshape(SB, SL, D)) + tuple(jnp.stack(o) for o in outs)
```

```python
import functools
import math

import jax
import jax.numpy as jnp
from jax import lax
from jax.experimental import pallas as pl
from jax.experimental.pallas import tpu as pltpu

F32 = jnp.float32
BF16 = jnp.bfloat16

D_MODEL = 1024
CHUNK = 64
D_SSM = 512
SSM_GROUP = 16
N_SSM_GROUPS = 32
SSM_STATE = 64
D_ATT = 512
N_HEADS = 4
HEAD_DIM = 64
V_DIM = 128
D_IN = D_SSM + 3 * D_ATT
N_MEM = 256
N_MEM_HEADS = 4
MEM_HEAD_DIM = 256
D_FF = 2816
EPS = 1e-6
LOG2E = math.log2(math.e)

N_STATE = N_SSM_GROUPS * SSM_STATE
SSM_BLOCKS = 2
BLK_CH = D_SSM // SSM_BLOCKS
BLK_ST = N_STATE // SSM_BLOCKS

LANES = 128
MXU_DIM = 256
FF_BOUNDS = tuple(range(0, D_FF, 3 * MXU_DIM)) + (D_FF,)
V7X_VMEM_LIMIT = 56 * 1024 * 1024

_NT = (((1,), (1,)), ((), ()))
_TN = (((0,), (0,)), ((), ()))


def _rmsnorm(x, g):
    return x * lax.rsqrt(jnp.mean(x * x, axis=-1, keepdims=True) + EPS) * g


def _dot(a, b):
    return jnp.dot(a, b, preferred_element_type=F32)


def _const_spec(shape):
    nd = len(shape)
    return pl.BlockSpec(shape, lambda *_: (0,) * nd, pipeline_mode=pl.Buffered(1))


def _swiglu_half(xn, wgu_ref, wd_ref):
    def gate_up(c):
        lo, hi = FF_BOUNDS[c], FF_BOUNDS[c + 1]
        return _dot(xn, wgu_ref[:, lo:hi]), _dot(xn, wgu_ref[:, D_FF + lo:D_FF + hi])

    n_chunks = len(FF_BOUNDS) - 1
    acc = None
    nxt = gate_up(0)
    for c in range(n_chunks):
        gate, up = nxt
        if c + 1 < n_chunks:
            nxt = gate_up(c + 1)
        act = (gate * jax.nn.sigmoid(gate) * up).astype(BF16)
        part = _dot(act, wd_ref[FF_BOUNDS[c]:FF_BOUNDS[c + 1], :])
        acc = part if acc is None else acc + part
    return acc


def _store_heads(ref, x, n_heads, width):
    for hh in range(n_heads):
        ref[:, hh, :] = x[:, hh * width:(hh + 1) * width]


def _ffn_mix_kernel(x_ref, g1_ref, wgu_ref, wd_ref, gm_ref, win_ref,
                    h_ref, u_ref, qb_ref, k_ref, v_ref, *kvb_refs):
    x = x_ref[...]
    xn = _rmsnorm(x, g1_ref[...]).astype(BF16)
    h = x + 0.5 * _swiglu_half(xn, wgu_ref, wd_ref)
    h_ref[...] = h
    hn = _rmsnorm(h, gm_ref[...]).astype(BF16)
    z = _dot(hn, win_ref[...])
    u_ref[...] = z[:, :D_SSM]
    q = z[:, D_SSM:D_SSM + D_ATT]
    k = z[:, D_SSM + D_ATT:D_SSM + 2 * D_ATT]
    v = z[:, D_SSM + 2 * D_ATT:]
    qb_ref[...] = (q * (HEAD_DIM ** -0.5 * LOG2E)).astype(BF16)
    _store_heads(k_ref, k, N_HEADS, V_DIM)
    _store_heads(v_ref, v, N_HEADS, V_DIM)
    if kvb_refs:
        kb_ref, vb_ref = kvb_refs
        kb_ref[...] = k.astype(BF16)
        vb_ref[...] = v.astype(BF16)


def _ffn_mix(x2d, g1, wgu, wd, gm, win, *, tm, bf16_kv):
    T = x2d.shape[0]
    tok = lambda w: pl.BlockSpec((tm, w), lambda i: (i, 0))
    heads = pl.BlockSpec((tm, N_HEADS, V_DIM), lambda i: (i, 0, 0))
    out_shape = [
        jax.ShapeDtypeStruct((T, D_MODEL), F32),
        jax.ShapeDtypeStruct((T, D_SSM), F32),
        jax.ShapeDtypeStruct((T, D_ATT), BF16),
        jax.ShapeDtypeStruct((T, N_HEADS, V_DIM), F32),
        jax.ShapeDtypeStruct((T, N_HEADS, V_DIM), F32),
    ]
    out_specs = [tok(D_MODEL), tok(D_SSM), tok(D_ATT), heads, heads]
    if bf16_kv:
        out_shape += [jax.ShapeDtypeStruct((T, D_ATT), BF16)] * 2
        out_specs += [tok(D_ATT)] * 2
    return pl.pallas_call(
        _ffn_mix_kernel,
        grid=(T // tm,),
        in_specs=[tok(D_MODEL), _const_spec((1, D_MODEL)), _const_spec(wgu.shape), _const_spec(wd.shape),
                  _const_spec((1, D_MODEL)), _const_spec(win.shape)],
        out_specs=tuple(out_specs),
        out_shape=tuple(out_shape),
        compiler_params=pltpu.CompilerParams(dimension_semantics=("arbitrary",),
                                             vmem_limit_bytes=V7X_VMEM_LIMIT),
        name="ffn_mix",
    )(x2d, g1, wgu, wd, gm, win)


def _s5_prep_kernel(are_ref, aim_ref, ldt_ref, bre_ref, bim_ref, abr_ref, abi_ref, bbr_ref, bbi_ref):
    lre = are_ref[...]
    lim = aim_ref[...]
    dt = jnp.exp(ldt_ref[...])
    mag = jnp.exp(lre * dt)
    ar = mag * jnp.cos(lim * dt)
    ai = mag * jnp.sin(lim * dt)
    abr_ref[...] = ar
    abi_ref[...] = ai
    den = lre * lre + lim * lim
    xr = ar - 1.0
    cr = (xr * lre + ai * lim) / den
    ci = (ai * lre - xr * lim) / den
    br = bre_ref[...]
    bi = bim_ref[...]
    bbr_ref[...] = cr * br - ci * bi
    bbi_ref[...] = cr * bi + ci * br


def _s5_prep(a_re, a_im, log_dt, b_re, b_im):
    G, P, H = N_SSM_GROUPS, SSM_STATE, SSM_GROUP
    per_row = lambda a: jnp.repeat(a, H, axis=0)
    b_re_t = jnp.swapaxes(b_re, 1, 2).reshape(G * H, P)
    b_im_t = jnp.swapaxes(b_im, 1, 2).reshape(G * H, P)
    o = jax.ShapeDtypeStruct((G * H, P), F32)
    abr, abi, bbr, bbi = pl.pallas_call(
        _s5_prep_kernel,
        out_shape=(o, o, o, o),
        name="s5_prep",
    )(per_row(a_re), per_row(a_im), per_row(log_dt.reshape(G, 1)), b_re_t, b_im_t)
    return (abr.reshape(G, H, P)[:, 0], abi.reshape(G, H, P)[:, 0], bbr.reshape(G, H, P), bbi.reshape(G, H, P))


def _block_diag_in(w_ghp):
    gpb = N_SSM_GROUPS // SSM_BLOCKS
    w = w_ghp.reshape(SSM_BLOCKS, gpb, SSM_GROUP, SSM_STATE)
    eye = jnp.eye(gpb, dtype=w.dtype)
    return jnp.einsum('jghp,gk->jghkp', w, eye).reshape(SSM_BLOCKS, BLK_CH, BLK_ST)


def _block_diag_out(w_ghp):
    gpb = N_SSM_GROUPS // SSM_BLOCKS
    w = w_ghp.reshape(SSM_BLOCKS, gpb, SSM_GROUP, SSM_STATE)
    eye = jnp.eye(gpb, dtype=w.dtype)
    return jnp.einsum('jghp,gk->jgpkh', w, eye).reshape(SSM_BLOCKS, BLK_ST, BLK_CH)


def _gelu_tanh(x):
    return 0.5 * x * (1.0 + jnp.tanh(math.sqrt(2.0 / math.pi) * (x + 0.044715 * (x * x * x))))


def _s5_kernel(u_ref, h0_ref, ab_ref, bbig_ref, cbig_ref, d_ref, wglu_ref, bglu_ref,
               y_ref, hl_ref, hst, abr, abi, io_sc, bu_sc, *, nb, tc, slab):
    c = pl.program_id(0)
    n_lt = D_SSM // LANES

    @pl.when(c == 0)
    def _():
        hst[...] = h0_ref[...]
        abr[...] = jnp.broadcast_to(ab_ref[0:1, :], (nb, N_STATE))
        abi[...] = jnp.broadcast_to(ab_ref[1:2, :], (nb, N_STATE))

    for b in range(nb):
        for lt in range(n_lt):
            io_sc[lt, pl.ds(b, tc, stride=nb), :] = u_ref[b, :, lt * LANES:(lt + 1) * LANES]
    u = jnp.concatenate([io_sc[lt] for lt in range(n_lt)], axis=1)
    ub = u.astype(BF16)
    for j in range(SSM_BLOCKS):
        bu_sc[j] = _dot(ub[:, BLK_CH * j:BLK_CH * (j + 1)], bbig_ref[j])
    ys = []
    for j in range(SSM_BLOCKS):
        for s in range(BLK_ST // slab):
            cr = slice(s * slab, (s + 1) * slab)
            ci = slice(BLK_ST + s * slab, BLK_ST + (s + 1) * slab)
            ar = abr[:, BLK_ST * j + s * slab:BLK_ST * j + (s + 1) * slab]
            ai = abi[:, BLK_ST * j + s * slab:BLK_ST * j + (s + 1) * slab]
            sr = slice(2 * BLK_ST * j + s * slab, 2 * BLK_ST * j + (s + 1) * slab)
            si = slice(2 * BLK_ST * j + BLK_ST + s * slab, 2 * BLK_ST * j + BLK_ST + (s + 1) * slab)
            hr, hi = hst[:, sr], hst[:, si]
            for t in range(tc):
                rows = slice(t * nb, (t + 1) * nb)
                hr, hi = (ar * hr - ai * hi + bu_sc[j, rows, cr], ar * hi + ai * hr + bu_sc[j, rows, ci])
                bu_sc[j, rows, cr] = hr
                bu_sc[j, rows, ci] = hi
            hst[:, sr] = hr
            hst[:, si] = hi
        ys.append(_dot(bu_sc[j].astype(BF16), cbig_ref[j]))
    y = jnp.concatenate(ys, axis=1) + d_ref[...] * u
    y = _gelu_tanh(y)
    gate = _dot(y.astype(BF16), wglu_ref[...]) + bglu_ref[...]
    out = y * jax.nn.sigmoid(gate)
    for lt in range(n_lt):
        io_sc[lt] = out[:, lt * LANES:(lt + 1) * LANES]
    for b in range(nb):
        for lt in range(n_lt):
            y_ref[b, :, lt * LANES:(lt + 1) * LANES] = io_sc[lt, pl.ds(b, tc, stride=nb), :].astype(BF16)

    @pl.when(c == pl.num_programs(0) - 1)
    def _():
        hl_ref[...] = hst[...]


def _s5(u, h0, ab, bbig, cbig, d, wglu, bglu, *, tc, slab):
    nb, seq, _ = u.shape
    rows = tc * nb
    io_spec = pl.BlockSpec((nb, tc, D_SSM), lambda c: (0, c, 0))
    return pl.pallas_call(
        functools.partial(_s5_kernel, nb=nb, tc=tc, slab=slab),
        grid=(seq // tc,),
        in_specs=[io_spec,
                  _const_spec((nb, 2 * N_STATE)), _const_spec((2, N_STATE)),
                  _const_spec(bbig.shape), _const_spec(cbig.shape),
                  _const_spec((1, D_SSM)), _const_spec((D_SSM, D_SSM)), _const_spec((1, D_SSM))],
        out_specs=(io_spec, pl.BlockSpec((nb, 2 * N_STATE), lambda c: (0, 0))),
        out_shape=(jax.ShapeDtypeStruct((nb, seq, D_SSM), BF16),
                   jax.ShapeDtypeStruct((nb, 2 * N_STATE), F32)),
        scratch_shapes=[pltpu.VMEM((nb, 2 * N_STATE), F32), pltpu.VMEM((nb, N_STATE), F32),
                        pltpu.VMEM((nb, N_STATE), F32), pltpu.VMEM((D_SSM // LANES, rows, LANES), F32),
                        pltpu.VMEM((SSM_BLOCKS, rows, 2 * BLK_ST), F32)],
        compiler_params=pltpu.CompilerParams(dimension_semantics=("arbitrary",),
                                             vmem_limit_bytes=V7X_VMEM_LIMIT),
        name="s5_scan",
    )(u, h0, ab, bbig, cbig, d, wglu, bglu)


def _state_to_cols(s_re, s_im):
    nb = s_re.shape[0]
    r = s_re.reshape(nb, SSM_BLOCKS, 1, BLK_ST)
    i = s_im.reshape(nb, SSM_BLOCKS, 1, BLK_ST)
    return jnp.concatenate([r, i], axis=2).reshape(nb, 2 * N_STATE)


def _cols_to_state(h):
    nb = h.shape[0]
    h4 = h.reshape(nb, SSM_BLOCKS, 2, BLK_ST)
    return (h4[:, :, 0].reshape(nb, N_SSM_GROUPS, SSM_STATE), h4[:, :, 1].reshape(nb, N_SSM_GROUPS, SSM_STATE))


def _two_map_queries(q):
    lane = lax.broadcasted_iota(jnp.int32, q.shape, 1)
    lo = lane < HEAD_DIM
    zero = jnp.zeros_like(q)
    return jnp.concatenate([jnp.where(lo, q, zero), jnp.where(lo, zero, q)], axis=0)


def _diff_lambda(lq_ref, lk_ref, lam_init):
    p = lq_ref[...] * lk_ref[...]
    s = jnp.sum(p, axis=1, keepdims=True)
    e = jnp.exp(s)
    return e[0:1, :] - e[1:2, :] + lam_init


def _softmax_update(s_tiles, shift, m_sc, l_sc):
    m_in = m_sc[...] - shift
    mx = jnp.concatenate([jnp.max(t, axis=0, keepdims=True) for t in s_tiles], axis=1)
    m_new = jnp.maximum(m_in, mx)
    alpha = jnp.exp2(m_in - m_new)
    p_tiles = [jnp.exp2(t - m_new[:, LANES * n:LANES * (n + 1)]) for n, t in enumerate(s_tiles)]
    l_sc[...] = alpha * l_sc[...] + jnp.concatenate([jnp.sum(p, axis=0, keepdims=True) for p in p_tiles], axis=1)
    m_sc[...] = m_new
    return jnp.concatenate(p_tiles, axis=1).astype(BF16), alpha


def _accumulate_values(vb, p, alpha, acc_sc):
    acc_sc[...] = alpha * acc_sc[...] + lax.dot_general(vb, p, _TN, preferred_element_type=F32)


def _emit_pipelined(n_steps, scores, softmax, values):
    s_next = scores(0)
    p_prev = None
    for t in range(n_steps):
        s_cur = s_next
        if t + 1 < n_steps:
            s_next = scores(t + 1)
        p_cur = softmax(t, s_cur)
        if p_prev is not None:
            values(t - 1, p_prev)
        p_prev = p_cur
    values(n_steps - 1, p_prev)


def _diff_finish(acc_t, l, n, lam, g, lam_init):
    o = (acc_t * (1.0 / l)).T
    return _rmsnorm(o[:n] - lam * o[n:], g) * (1.0 - lam_init)


def _in_block_bias(n_keys, n_queries, slope2):
    kl = lax.broadcasted_iota(jnp.int32, (n_keys, n_queries), 0)
    ql = lax.broadcasted_iota(jnp.int32, (n_keys, n_queries), 1)
    return kl, ql, slope2 * (ql - jnp.abs(ql - kl)).astype(F32)


def _attn_prompt_kernel(q_lo_ref, q_hi_ref, k_ref, v_ref, lq_ref, lk_ref, g_ref, *rest, i_lo, i_hi, bq, lam_init):
    o_ref, qq_sc, cb_sc, db_sc, m_sc, l_sc, acc_sc = rest[-7:]
    slopes2 = [2.0 ** (-8.0 * (h + 1) / N_HEADS) * LOG2E for h in range(N_HEADS)]
    cols = [slice(h * V_DIM, (h + 1) * V_DIM) for h in range(N_HEADS)]
    n_tiles = 2 * bq // LANES
    q_tiles = n_tiles // 2

    @pl.when(pl.program_id(0) == 0)
    def _():
        krow = lax.broadcasted_iota(jnp.int32, (bq, LANES), 0).astype(F32)
        kl, ql, rel = _in_block_bias(bq, bq, 1.0)
        visible = (kl // CHUNK) <= (ql // CHUNK)
        for h in range(N_HEADS):
            cb_sc[h] = slopes2[h] * krow
            db_sc[h] = jnp.where(visible, slopes2[h] * rel, -jnp.inf)

    for sel, q_ref in enumerate((q_lo_ref, q_hi_ref)):
        for h in range(N_HEADS):
            qq_sc[sel, h] = _two_map_queries(q_ref[0, :, cols[h]])
    m_sc[...] = jnp.full(m_sc.shape, -jnp.inf, F32)
    l_sc[...] = jnp.zeros(l_sc.shape, F32)
    acc_sc[...] = jnp.zeros(acc_sc.shape, F32)

    steps = [(0, j, j == i_lo) for j in range(i_lo + 1)] + [(1, j, j == i_hi) for j in range(i_hi + 1)]

    def scores(t):
        sel, j, _ = steps[t]
        return [lax.dot_general(k_ref[0, j * bq:(j + 1) * bq, cols[h]], qq_sc[sel, h], _NT,
                                preferred_element_type=F32) for h in range(N_HEADS)]

    def softmax(t, s_heads):
        sel, _, diag = steps[t]
        out = []
        for h, s in enumerate(s_heads):
            if diag:
                tiles = [s[:, LANES * n:LANES * (n + 1)]
                         + db_sc[h, :, LANES * (n % q_tiles):LANES * (n % q_tiles + 1)] for n in range(n_tiles)]
            else:
                tiles = [s[:, LANES * n:LANES * (n + 1)] + cb_sc[h] for n in range(n_tiles)]
            out.append(_softmax_update(tiles, slopes2[h] * bq, m_sc.at[sel, h], l_sc.at[sel, h]))
        return out

    def values(t, p_heads):
        sel, j, _ = steps[t]
        for h, (p, alpha) in enumerate(p_heads):
            _accumulate_values(v_ref[0, j * bq:(j + 1) * bq, cols[h]], p, alpha, acc_sc.at[sel, h])

    _emit_pipelined(len(steps), scores, softmax, values)
    lam = _diff_lambda(lq_ref, lk_ref, lam_init)
    for sel in range(2):
        for h in range(N_HEADS):
            o = _diff_finish(acc_sc[sel, h], l_sc[sel, h], bq, lam, g_ref[...], lam_init)
            o_ref[0, 0, sel, :, cols[h]] = o.astype(o_ref.dtype)


def _attn_prompt(qb, kb, vb, lq, lk, g, *, lam_init, bq=256):
    B, L, _ = qb.shape
    n = L // bq
    small = lambda shape: pl.BlockSpec(shape, lambda b: (0, 0))
    out = None
    for i_lo in range(n // 2):
        i_hi = n - 1 - i_lo
        keys = pl.BlockSpec((1, (i_hi + 1) * bq, D_ATT), lambda b: (b, 0, 0))
        in_specs = [pl.BlockSpec((1, bq, D_ATT), lambda b, i=i_lo: (b, i, 0)),
                    pl.BlockSpec((1, bq, D_ATT), lambda b, i=i_hi: (b, i, 0)),
                    keys, keys, small((2, HEAD_DIM)), small((2, HEAD_DIM)), small((1, V_DIM))]
        operands = [qb, qb, kb, vb, lq, lk, g]
        aliases = {}
        if out is not None:
            aliases = {len(operands): 0}
            in_specs.append(pl.BlockSpec(memory_space=pl.ANY))
            operands.append(out)
        out = pl.pallas_call(
            functools.partial(_attn_prompt_kernel, i_lo=i_lo, i_hi=i_hi, bq=bq, lam_init=lam_init),
            grid=(B,),
            in_specs=in_specs,
            out_specs=pl.BlockSpec((1, 1, 2, bq, D_ATT), lambda b, i=i_lo: (i, b, 0, 0, 0)),
            out_shape=jax.ShapeDtypeStruct((n // 2, B, 2, bq, D_ATT), BF16),
            scratch_shapes=[pltpu.VMEM((2, N_HEADS, 2 * bq, V_DIM), BF16),
                            pltpu.VMEM((N_HEADS, bq, LANES), F32), pltpu.VMEM((N_HEADS, bq, bq), F32),
                            pltpu.VMEM((2, N_HEADS, 1, 2 * bq), F32), pltpu.VMEM((2, N_HEADS, 1, 2 * bq), F32),
                            pltpu.VMEM((2, N_HEADS, V_DIM, 2 * bq), F32)],
            input_output_aliases=aliases,
            compiler_params=pltpu.CompilerParams(dimension_semantics=("arbitrary",),
                                                 vmem_limit_bytes=V7X_VMEM_LIMIT),
            name=f"attn_prompt_{i_lo}",
        )(*operands)
    return out


def _attn_sample_kernel(q_ref, kc_ref, vc_ref, kn_ref, vn_ref, lq_ref, lk_ref, g_ref, o_ref,
                        qq_sc, m_sc, l_sc, acc_sc, *, nq, bk, lam_init):
    slopes2 = [2.0 ** (-8.0 * (h + 1) / N_HEADS) * LOG2E for h in range(N_HEADS)]
    n_cache = kc_ref.shape[1] // (bk * N_HEADS)
    for h in range(N_HEADS):
        qq_sc[h] = _two_map_queries(q_ref[0, :, h * V_DIM:(h + 1) * V_DIM])
    m_sc[...] = jnp.full(m_sc.shape, -jnp.inf, F32)
    l_sc[...] = jnp.zeros(l_sc.shape, F32)
    acc_sc[...] = jnp.zeros(acc_sc.shape, F32)

    krow = lax.broadcasted_iota(jnp.int32, (bk, 2 * nq), 0).astype(F32)
    kl = lax.broadcasted_iota(jnp.int32, (nq, 2 * nq), 0)
    lane = lax.broadcasted_iota(jnp.int32, (nq, 2 * nq), 1)
    ql = jnp.where(lane >= nq, lane - nq, lane)
    rel_new = (ql - jnp.abs(ql - kl)).astype(F32)

    def head_rows(t, h, cache_ref, new_ref):
        if t < n_cache:
            return cache_ref[0, pl.ds(t * bk * N_HEADS + h, bk, stride=N_HEADS), :].astype(BF16)
        return new_ref[0, pl.ds(h, nq, stride=N_HEADS), :].astype(BF16)

    def scores(t):
        return [lax.dot_general(head_rows(t, h, kc_ref, kn_ref), qq_sc[h], _NT, preferred_element_type=F32)
                for h in range(N_HEADS)]

    def softmax(t, s_heads):
        rel = krow if t < n_cache else rel_new
        return [_softmax_update([s + slopes2[h] * rel], slopes2[h] * bk, m_sc.at[h], l_sc.at[h])
                for h, s in enumerate(s_heads)]

    def values(t, p_heads):
        for h, (p, alpha) in enumerate(p_heads):
            _accumulate_values(head_rows(t, h, vc_ref, vn_ref), p, alpha, acc_sc.at[h])

    _emit_pipelined(n_cache + 1, scores, softmax, values)
    lam = _diff_lambda(lq_ref, lk_ref, lam_init)
    for h in range(N_HEADS):
        o = _diff_finish(acc_sc[h], l_sc[h], nq, lam, g_ref[...], lam_init)
        o_ref[0, :, h * V_DIM:(h + 1) * V_DIM] = o.astype(o_ref.dtype)


def _attn_sample(qb, k_cache, v_cache, k_new, v_new, lq, lk, g, *, lam_init, bk=1024):
    B, nq, _ = qb.shape
    q_spec = pl.BlockSpec((1, nq, D_ATT), lambda b: (b, 0, 0))
    new_spec = pl.BlockSpec((1, nq * N_HEADS, V_DIM), lambda b: (b, 0, 0))
    cache_spec = pl.BlockSpec((1, k_cache.shape[1], V_DIM), lambda b: (b, 0, 0))
    small = lambda shape: pl.BlockSpec(shape, lambda b: (0, 0))
    return pl.pallas_call(
        functools.partial(_attn_sample_kernel, nq=nq, bk=bk, lam_init=lam_init),
        grid=(B,),
        in_specs=[q_spec, cache_spec, cache_spec, new_spec, new_spec,
                  small((2, HEAD_DIM)), small((2, HEAD_DIM)), small((1, V_DIM))],
        out_specs=q_spec,
        out_shape=jax.ShapeDtypeStruct((B, nq, D_ATT), BF16),
        scratch_shapes=[pltpu.VMEM((N_HEADS, 2 * nq, V_DIM), BF16), pltpu.VMEM((N_HEADS, 1, 2 * nq), F32),
                        pltpu.VMEM((N_HEADS, 1, 2 * nq), F32), pltpu.VMEM((N_HEADS, V_DIM, 2 * nq), F32)],
        compiler_params=pltpu.CompilerParams(dimension_semantics=("arbitrary",),
                                             vmem_limit_bytes=V7X_VMEM_LIMIT),
        name="attn_sample",
    )(qb, k_cache, v_cache, k_new, v_new, lq, lk, g)


def _mem_kv_kernel(m_ref, g_ref, wck_ref, wcv_ref, mk_ref, mv_ref, mkb_ref, mvb_ref):
    mn = _rmsnorm(m_ref[...], g_ref[...]).astype(BF16)
    k = _dot(mn, wck_ref[...])
    v = _dot(mn, wcv_ref[...])
    for s in range(mk_ref.shape[0]):
        rows = slice(s * N_MEM, (s + 1) * N_MEM)
        _store_heads(mk_ref.at[s], k[rows], N_MEM_HEADS, MEM_HEAD_DIM)
        _store_heads(mv_ref.at[s], v[rows], N_MEM_HEADS, MEM_HEAD_DIM)
        for hh in range(N_MEM_HEADS):
            cols = slice(hh * MEM_HEAD_DIM, (hh + 1) * MEM_HEAD_DIM)
            mkb_ref[s, hh] = k[rows, cols].astype(BF16)
            mvb_ref[s, hh] = v[rows, cols].astype(BF16)


def _mem_kv(mem2d, g, wck, wcv, *, tm=512):
    T = mem2d.shape[0]
    spt = tm // N_MEM
    tok = pl.BlockSpec((tm, D_MODEL), lambda i: (i, 0))
    heads = pl.BlockSpec((spt, N_MEM, N_MEM_HEADS, MEM_HEAD_DIM), lambda i: (i, 0, 0, 0))
    head_major = pl.BlockSpec((spt, N_MEM_HEADS, N_MEM, MEM_HEAD_DIM), lambda i: (i, 0, 0, 0))
    f = jax.ShapeDtypeStruct((T // N_MEM, N_MEM, N_MEM_HEADS, MEM_HEAD_DIM), F32)
    b = jax.ShapeDtypeStruct((T // N_MEM, N_MEM_HEADS, N_MEM, MEM_HEAD_DIM), BF16)
    return pl.pallas_call(
        _mem_kv_kernel,
        grid=(T // tm,),
        in_specs=[tok, _const_spec((1, D_MODEL)), _const_spec(wck.shape), _const_spec(wcv.shape)],
        out_specs=(heads, heads, head_major, head_major),
        out_shape=(f, f, b, b),
        compiler_params=pltpu.CompilerParams(dimension_semantics=("arbitrary",),
                                             vmem_limit_bytes=V7X_VMEM_LIMIT),
        name="mem_kv",
    )(mem2d, g, wck, wcv)


def _tail_kernel(h_ref, ys_ref, *refs, n_ya, nb, rpb, final_norm):
    ya_refs = refs[:n_ya]
    mk_ref, mv_ref, wout_ref, gc_ref, wcq_ref, wco_ref, g2_ref, wgu_ref, wd_ref, gf_ref, y_ref = refs[n_ya:]
    rows_per = h_ref.shape[0] // n_ya
    ya = [r[...].reshape(rows_per, D_ATT) for r in ya_refs]
    ya = ya[0] if n_ya == 1 else jnp.concatenate(ya, axis=0)
    h = h_ref[...] + _dot(ys_ref[...], wout_ref[:D_SSM, :]) + _dot(ya, wout_ref[D_SSM:, :])

    qn = _rmsnorm(h, gc_ref[...]).astype(BF16)
    q = (_dot(qn, wcq_ref[...]) * (MEM_HEAD_DIM ** -0.5 * LOG2E)).astype(BF16)
    rows = []
    for lb in range(nb):
        heads = []
        for hh in range(N_MEM_HEADS):
            cols = slice(hh * MEM_HEAD_DIM, (hh + 1) * MEM_HEAD_DIM)
            qh = q[lb * rpb:(lb + 1) * rpb, cols]
            s = lax.dot_general(qh, mk_ref[lb, hh], _NT, preferred_element_type=F32)
            p = jnp.exp2(s - jnp.max(s, axis=1, keepdims=True))
            o = _dot(p.astype(BF16), mv_ref[lb, hh]) / jnp.sum(p, axis=1, keepdims=True)
            heads.append(o)
        rows.append(jnp.concatenate(heads, axis=1))
    o = rows[0] if nb == 1 else jnp.concatenate(rows, axis=0)
    h = h + _dot(o.astype(BF16), wco_ref[...])

    xn = _rmsnorm(h, g2_ref[...]).astype(BF16)
    x2 = h + 0.5 * _swiglu_half(xn, wgu_ref, wd_ref)
    y_ref[...] = _rmsnorm(x2, gf_ref[...]) if final_norm else x2


def _tail(h2d, ys2d, ya, ya_specs, mkb, mvb, wout, gc, wcq, wco, g2, wgu, wd, gf, *, seq, tm, final_norm):
    T = h2d.shape[0]
    if seq >= tm:
        nb, rpb = 1, tm
        nt = seq // tm
        mem_map = lambda i: (i // nt, 0, 0, 0)
    else:
        nb, rpb = tm // seq, seq
        mem_map = lambda i: (i, 0, 0, 0)
    mem_spec = pl.BlockSpec((nb, N_MEM_HEADS, N_MEM, MEM_HEAD_DIM), mem_map)
    tok = lambda w: pl.BlockSpec((tm, w), lambda i: (i, 0))
    return pl.pallas_call(
        functools.partial(_tail_kernel, n_ya=len(ya_specs), nb=nb, rpb=rpb, final_norm=final_norm),
        grid=(T // tm,),
        in_specs=[tok(D_MODEL), tok(D_SSM), *ya_specs, mem_spec, mem_spec,
                  _const_spec(wout.shape), _const_spec((1, D_MODEL)), _const_spec(wcq.shape),
                  _const_spec(wco.shape), _const_spec((1, D_MODEL)), _const_spec(wgu.shape),
                  _const_spec(wd.shape), _const_spec((1, D_MODEL))],
        out_specs=tok(D_MODEL),
        out_shape=jax.ShapeDtypeStruct((T, D_MODEL), F32),
        compiler_params=pltpu.CompilerParams(dimension_semantics=("arbitrary",),
                                             vmem_limit_bytes=V7X_VMEM_LIMIT),
        name="tail",
    )(h2d, ys2d, *([ya] * len(ya_specs)), mkb, mvb, wout, gc, wcq, wco, g2, wgu, wd, gf)


def _paired_block_specs(n, bq, tm, tiles_per_stream):
    per_tile = tm // bq

    def spec(r):
        def index(i):
            j = (i % tiles_per_stream) * per_tile + r
            return (jnp.minimum(j, n - 1 - j), i // tiles_per_stream, (j >= n // 2).astype(jnp.int32), 0, 0)
        return pl.BlockSpec((1, 1, 1, bq, D_ATT), index)

    return [spec(r) for r in range(per_tile)]


def kernel(x_prompt, x_sample, cache_attn_k, cache_attn_v, state_s5_re, state_s5_im, cache_mem_k, cache_mem_v, mem_prompt, g_ffn1, w_ffn1_gu, w_ffn1_d, g_mix, w_in, ssm_a_re, ssm_a_im, ssm_log_dt, ssm_b_re, ssm_b_im, ssm_c_re, ssm_c_im, ssm_d, w_glu, b_glu, lambda_q, lambda_k, g_subln, w_out, g_mem, g_cross, w_cq, w_ck, w_cv, w_co, g_ffn2, w_ffn2_gu, w_ffn2_d, g_final):
    depth = g_ffn1.shape[0]
    B, L, D = x_prompt.shape
    SB, SL, _ = x_sample.shape
    row = lambda a: a.reshape(1, -1)
    tm, bq = 512, 256

    xp = x_prompt.reshape(B * L, D)
    xs = x_sample.reshape(SB * SL, D)
    outs = [[] for _ in range(10)]
    for l in range(depth):
        lam_init = 0.8 - 0.6 * math.exp(-0.3 * l)
        wgu1, wd1, win = w_ffn1_gu[l].astype(BF16), w_ffn1_d[l].astype(BF16), w_in[l].astype(BF16)
        wgu2, wd2 = w_ffn2_gu[l].astype(BF16), w_ffn2_d[l].astype(BF16)
        wout, wcq, wco = w_out[l].astype(BF16), w_cq[l].astype(BF16), w_co[l].astype(BF16)
        wck, wcv, wglu = w_ck[l].astype(BF16), w_cv[l].astype(BF16), w_glu[l].astype(BF16)

        abr, abi, bbr, bbi = _s5_prep(ssm_a_re[l], ssm_a_im[l], ssm_log_dt[l], ssm_b_re[l], ssm_b_im[l])
        ab = jnp.stack([abr.reshape(N_STATE), abi.reshape(N_STATE)])
        bbig = jnp.concatenate([_block_diag_in(bbr), _block_diag_in(bbi)], axis=2).astype(BF16)
        cbig = jnp.concatenate([_block_diag_out(ssm_c_re[l]), -_block_diag_out(ssm_c_im[l])], axis=1).astype(BF16)
        s5_w = (ab, bbig, cbig, row(ssm_d[l]), wglu, row(b_glu[l]))
        att_w = (lambda_q[l], lambda_k[l], row(g_subln[l]))
        tail_w = (wout, row(g_cross[l]), wcq, wco, row(g_ffn2[l]), wgu2, wd2, row(g_final))
        last = l == depth - 1

        hp, up, qp, kp, vp, kpb, vpb = _ffn_mix(xp, row(g_ffn1[l]), wgu1, wd1, row(g_mix[l]), win,
                                                 tm=tm, bf16_kv=True)
        zeros = jnp.zeros((B, 2 * N_STATE), F32)
        yp_ssm, hlp = _s5(up.reshape(B, L, D_SSM), zeros, *s5_w, tc=CHUNK, slab=512)
        yp_att = _attn_prompt(qp.reshape(B, L, D_ATT), kpb.reshape(B, L, D_ATT), vpb.reshape(B, L, D_ATT),
                              *att_w, lam_init=lam_init, bq=bq)
        mkp, mvp, mkpb, mvpb = _mem_kv(mem_prompt.reshape(B * N_MEM, D), row(g_mem[l]), wck, wcv)
        xp = _tail(hp, yp_ssm.reshape(B * L, D_SSM), yp_att, _paired_block_specs(L // bq, bq, tm, L // tm),
                   mkpb, mvpb, *tail_w, seq=L, tm=tm, final_norm=last)

        hs, us, qs, ks, vs = _ffn_mix(xs, row(g_ffn1[l]), wgu1, wd1, row(g_mix[l]), win,
                                      tm=tm, bf16_kv=False)
        ks = ks.reshape(SB, SL, N_HEADS, V_DIM)
        vs = vs.reshape(SB, SL, N_HEADS, V_DIM)
        h0 = _state_to_cols(state_s5_re[l], state_s5_im[l])
        ys_ssm, hls = _s5(us.reshape(SB, SL, D_SSM), h0, *s5_w, tc=SL, slab=256)
        key_head_rows = lambda a: a.reshape(SB, -1, V_DIM)
        ys_att = _attn_sample(qs.reshape(SB, SL, D_ATT), key_head_rows(cache_attn_k[l]),
                              key_head_rows(cache_attn_v[l]), key_head_rows(ks), key_head_rows(vs),
                              *att_w, lam_init=lam_init)
        xs = _tail(hs, ys_ssm.reshape(SB * SL, D_SSM), ys_att.reshape(SB * SL, D_ATT),
                   [pl.BlockSpec((tm // 2, D_ATT), lambda i: (i, 0))],
                   jnp.swapaxes(cache_mem_k[l], 1, 2).astype(BF16),
                   jnp.swapaxes(cache_mem_v[l], 1, 2).astype(BF16), *tail_w,
                   seq=SL, tm=tm // 2, final_norm=last)

        rep, imp = _cols_to_state(hlp)
        res, ims = _cols_to_state(hls)
        layer = (kp.reshape(B, L, N_HEADS, V_DIM), vp.reshape(B, L, N_HEADS, V_DIM), rep, imp,
                 mkp, mvp,
                 ks, vs, res, ims)
        for o, a in zip(outs, layer):
            o.append(a)

    return (xp.reshape(B, L, D), xs.reshape(SB, SL, D)) + tuple(jnp.stack(o) for o in outs)
```
